```python
import math
import jax, jax.numpy as jnp
from jax import lax
import numpy as np

D_MODEL = 1024
BATCH = 16
SEQ = 4096
DEPTH = 1

SWA_Q_HEADS = 8
SWA_KV_HEADS = 2
SWA_GROUP = SWA_Q_HEADS // SWA_KV_HEADS
FOX_HEADS = 8
HEAD_DIM = D_MODEL // (SWA_Q_HEADS + FOX_HEADS)
D_SWA = SWA_Q_HEADS * HEAD_DIM
D_SWA_KV = SWA_KV_HEADS * HEAD_DIM
D_FOX = FOX_HEADS * HEAD_DIM
D_MIX = D_SWA + D_FOX
WINDOW = 128
BLOCK = 128
D_FF = 2816
MACARON = 0.5
ALPHA = (2.0 * DEPTH) ** 0.25
BETA = (8.0 * DEPTH) ** -0.25
LN_EPS = 1e-5
RMS_EPS = 1e-6
N_MODS = 9
IN_SPLITS = (D_SWA, D_SWA_KV, D_SWA_KV, D_FOX, D_FOX, D_FOX, FOX_HEADS)
D_IN = D_SWA + 2 * D_SWA_KV + 3 * D_FOX + FOX_HEADS

kernel_name = "hymba_swa_sink_fox_macaron_deepnorm_adaln"


def layer_norm(x, g, b):
    xf = x.astype(jnp.float32)
    mu = jnp.mean(xf, axis=-1, keepdims=True)
    var = jnp.mean(jnp.square(xf - mu), axis=-1, keepdims=True)
    y = (xf - mu) * lax.rsqrt(var + LN_EPS) * g.astype(jnp.float32) + b.astype(jnp.float32)
    return y.astype(x.dtype)


def rms_norm(x, g):
    xf = x.astype(jnp.float32)
    y = xf * lax.rsqrt(jnp.mean(jnp.square(xf), axis=-1, keepdims=True) + RMS_EPS)
    return (y * g.astype(jnp.float32)).astype(x.dtype)


def modulate(x, shift, scale):
    return x * (1 + scale) + shift


def swiglu(h, w_gate, w_up, w_down):
    return (jax.nn.silu(h @ w_gate) * (h @ w_up)) @ w_down


def alibi_slopes(n_heads):
    return jnp.exp2(-8.0 * jnp.arange(1, n_heads + 1, dtype=jnp.float32) / n_heads)


def swa_sink_attention(q, k, v, sinks):
    B, S = q.shape[0], q.shape[1]
    nb = S // BLOCK
    qb = q.reshape(B, nb, BLOCK, SWA_KV_HEADS, SWA_GROUP, HEAD_DIM)
    kb = k.reshape(B, nb, BLOCK, SWA_KV_HEADS, HEAD_DIM)
    vb = v.reshape(B, nb, BLOCK, SWA_KV_HEADS, HEAD_DIM)
    pad_k = jnp.zeros_like(kb[:, :1])
    pad_v = jnp.zeros_like(vb[:, :1])
    kk = jnp.concatenate([jnp.concatenate([pad_k, kb[:, :-1]], axis=1), kb], axis=2)
    vv = jnp.concatenate([jnp.concatenate([pad_v, vb[:, :-1]], axis=1), vb], axis=2)
    scale = 1.0 / math.sqrt(HEAD_DIM)
    s = jnp.einsum('bnqkgd,bnskd->bnkgqs', qb, kk).astype(jnp.float32) * scale
    qpos = jnp.arange(BLOCK)
    kpos = jnp.arange(2 * BLOCK) - BLOCK
    dist = qpos[:, None] - kpos[None, :]
    valid_key = (jnp.arange(nb)[:, None] * BLOCK + kpos[None, :]) >= 0
    mask = (dist >= 0)[None] & (dist < WINDOW)[None] & valid_key[:, None, :]
    slopes = alibi_slopes(SWA_Q_HEADS).reshape(SWA_KV_HEADS, SWA_GROUP)
    alibi = slopes[:, :, None, None] * dist.astype(jnp.float32)
    s = jnp.where(mask[None, :, None, None], s - alibi, -jnp.inf)
    sink = sinks.astype(jnp.float32).reshape(SWA_KV_HEADS, SWA_GROUP)[None, None, :, :, None]
    m = jnp.maximum(jnp.max(s, axis=-1), sink)
    p = jnp.exp(s - m[..., None])
    denom = jnp.sum(p, axis=-1) + jnp.exp(sink - m)
    p = (p / denom[..., None]).astype(v.dtype)
    o = jnp.einsum('bnkgqs,bnskd->bnqkgd', p, vv)
    return o.reshape(B, S, SWA_Q_HEADS * HEAD_DIM)


def forgetting_attention(q, k, v, log_f):
    B, S = q.shape[0], q.shape[1]
    nb = S // BLOCK
    cum = jnp.cumsum(log_f, axis=1).transpose(0, 2, 1)
    q_blocks = q.reshape(B, nb, BLOCK, FOX_HEADS, HEAD_DIM).transpose(1, 0, 2, 3, 4)
    c_blocks = cum.reshape(B, FOX_HEADS, nb, BLOCK).transpose(2, 0, 1, 3)
    kpos = jnp.arange(S)
    scale = 1.0 / math.sqrt(HEAD_DIM)

    def one_block(args):
        i, qb, cq = args
        s = jnp.einsum('bqhd,bshd->bhqs', qb, k).astype(jnp.float32) * scale
        s = s + cq[..., :, None] - cum[..., None, :]
        tpos = i * BLOCK + jnp.arange(BLOCK)
        causal = kpos[None, :] <= tpos[:, None]
        s = jnp.where(causal[None, None], s, -jnp.inf)
        p = jax.nn.softmax(s, axis=-1).astype(v.dtype)
        return jnp.einsum('bhqs,bshd->bqhd', p, v)

    o = lax.map(one_block, (jnp.arange(nb), q_blocks, c_blocks))
    return o.transpose(1, 0, 2, 3, 4).reshape(B, S, FOX_HEADS * HEAD_DIM)


def token_mix(h, w_in, b_forget, swa_sinks, grp_gain, w_out):
    B, S = h.shape[0], h.shape[1]
    proj = h @ w_in
    idx, acc = [], 0
    for n in IN_SPLITS[:-1]:
        acc += n
        idx.append(acc)
    q_a, k_a, v_a, q_b, k_b, v_b, f_logit = jnp.split(proj, idx, axis=-1)
    o_a = swa_sink_attention(q_a.reshape(B, S, SWA_Q_HEADS, HEAD_DIM),
                             k_a.reshape(B, S, SWA_KV_HEADS, HEAD_DIM),
                             v_a.reshape(B, S, SWA_KV_HEADS, HEAD_DIM), swa_sinks)
    log_f = jax.nn.log_sigmoid(f_logit.astype(jnp.float32) + b_forget.astype(jnp.float32))
    o_b = forgetting_attention(q_b.reshape(B, S, FOX_HEADS, HEAD_DIM),
                               k_b.reshape(B, S, FOX_HEADS, HEAD_DIM),
                               v_b.reshape(B, S, FOX_HEADS, HEAD_DIM), log_f)
    o = jnp.concatenate([rms_norm(o_a, grp_gain[:D_SWA]), rms_norm(o_b, grp_gain[D_SWA:])], axis=-1)
    return o @ w_out


def setup_inputs(seed: int = 0) -> dict:
    key = jax.random.key(seed)
    ks = jax.random.split(key, 32)
    f32 = jnp.float32
    L, D = DEPTH, D_MODEL
    nrm = lambda k, shape: jax.random.normal(k, shape, f32)
    x = nrm(ks[0], (BATCH, SEQ, D))
    c = nrm(ks[1], (BATCH, D))
    w_ada = nrm(ks[2], (L, D, N_MODS * D)) * (0.5 * D ** -0.5)
    b_ada = nrm(ks[3], (L, N_MODS * D)) * 0.01
    ffn1_w_gate = nrm(ks[4], (L, D, D_FF)) * D ** -0.5
    ffn1_w_up = nrm(ks[5], (L, D, D_FF)) * D ** -0.5
    ffn1_w_down = nrm(ks[6], (L, D_FF, D)) * (BETA * D_FF ** -0.5)
    w_q_a = nrm(ks[7], (L, D, D_SWA)) * D ** -0.5
    w_k_a = nrm(ks[8], (L, D, D_SWA_KV)) * D ** -0.5
    w_v_a = nrm(ks[9], (L, D, D_SWA_KV)) * (BETA * D ** -0.5)
    w_q_b = nrm(ks[10], (L, D, D_FOX)) * D ** -0.5
    w_k_b = nrm(ks[11], (L, D, D_FOX)) * D ** -0.5
    w_v_b = nrm(ks[12], (L, D, D_FOX)) * (BETA * D ** -0.5)
    w_f = nrm(ks[13], (L, D, FOX_HEADS)) * D ** -0.5
    w_in = jnp.concatenate([w_q_a, w_k_a, w_v_a, w_q_b, w_k_b, w_v_b, w_f], axis=-1)
    b_forget = 2.0 + 0.5 * nrm(ks[14], (L, FOX_HEADS))
    swa_sinks = 0.5 * nrm(ks[15], (L, SWA_Q_HEADS))
    grp_gain = 1.0 + 0.02 * nrm(ks[16], (L, D_MIX))
    w_out = nrm(ks[17], (L, D_MIX, D)) * (BETA * D_MIX ** -0.5)
    ffn2_w_gate = nrm(ks[18], (L, D, D_FF)) * D ** -0.5
    ffn2_w_up = nrm(ks[19], (L, D, D_FF)) * D ** -0.5
    ffn2_w_down = nrm(ks[20], (L, D_FF, D)) * (BETA * D_FF ** -0.5)
    ln1_g = 1.0 + 0.02 * nrm(ks[21], (L, D))
    ln1_b = 0.02 * nrm(ks[22], (L, D))
    ln2_g = 1.0 + 0.02 * nrm(ks[23], (L, D))
    ln2_b = 0.02 * nrm(ks[24], (L, D))
    ln3_g = 1.0 + 0.02 * nrm(ks[25], (L, D))
    ln3_b = 0.02 * nrm(ks[26], (L, D))
    return {"x": x, "c": c, "w_ada": w_ada, "b_ada": b_ada,
            "ffn1_w_gate": ffn1_w_gate, "ffn1_w_up": ffn1_w_up, "ffn1_w_down": ffn1_w_down,
            "w_in": w_in, "b_forget": b_forget, "swa_sinks": swa_sinks,
            "grp_gain": grp_gain, "w_out": w_out,
            "ffn2_w_gate": ffn2_w_gate, "ffn2_w_up": ffn2_w_up, "ffn2_w_down": ffn2_w_down,
            "ln1_g": ln1_g, "ln1_b": ln1_b, "ln2_g": ln2_g, "ln2_b": ln2_b,
            "ln3_g": ln3_g, "ln3_b": ln3_b}


def reference(x, c, w_ada, b_ada, ffn1_w_gate, ffn1_w_up, ffn1_w_down, w_in, b_forget,
              swa_sinks, grp_gain, w_out, ffn2_w_gate, ffn2_w_up, ffn2_w_down,
              ln1_g, ln1_b, ln2_g, ln2_b, ln3_g, ln3_b):
    silu_c = jax.nn.silu(c)
    for l in range(DEPTH):
        mods = silu_c @ w_ada[l] + b_ada[l]
        sh1, sc1, g1, sh2, sc2, g2, sh3, sc3, g3 = [m[:, None, :] for m in jnp.split(mods, N_MODS, axis=-1)]
        h = modulate(x, sh1, sc1)
        y = swiglu(h, ffn1_w_gate[l], ffn1_w_up[l], ffn1_w_down[l])
        x = layer_norm(ALPHA * x + (1 + g1) * (MACARON * y), ln1_g[l], ln1_b[l])
        h = modulate(x, sh2, sc2)
        y = token_mix(h, w_in[l], b_forget[l], swa_sinks[l], grp_gain[l], w_out[l])
        x = layer_norm(ALPHA * x + (1 + g2) * y, ln2_g[l], ln2_b[l])
        h = modulate(x, sh3, sc3)
        y = swiglu(h, ffn2_w_gate[l], ffn2_w_up[l], ffn2_w_down[l])
        x = layer_norm(ALPHA * x + (1 + g3) * (MACARON * y), ln3_g[l], ln3_b[l])
    return x
```

```python
import functools
import math

import jax
import jax.numpy as jnp
from jax import lax
from jax.experimental import pallas as pl
from jax.experimental.pallas import tpu as pltpu

F32 = jnp.float32
BF16 = jnp.bfloat16

SWA_Q_HEADS = 8
SWA_KV_HEADS = 2
SWA_GROUP = SWA_Q_HEADS // SWA_KV_HEADS
FOX_HEADS = 8
HEAD_DIM = 64
D_SWA = SWA_Q_HEADS * HEAD_DIM
D_SWA_KV = SWA_KV_HEADS * HEAD_DIM
D_FOX = FOX_HEADS * HEAD_DIM
WINDOW = 128
BLOCK = 128
MACARON = 0.5
DEPTH = 1
ALPHA = (2.0 * DEPTH) ** 0.25
LN_EPS = 1e-5
RMS_EPS = 1e-6
N_MODS = 9
QK_SCALE = 1.0 / math.sqrt(HEAD_DIM)

LANES = 128
MXU_DIM = 256
VMEM_LIMIT = 56 * 1024 * 1024

NEG_BIG = -1e30

FFN_TM = 512
FFN_TF = 256
PROJ_TM = 512
ATT_TQ = 256
ATT_TK = 256


def _layer_norm(z, g, b):
    mu = jnp.mean(z, axis=-1, keepdims=True)
    zc = z - mu
    var = jnp.mean(zc * zc, axis=-1, keepdims=True)
    return zc * lax.rsqrt(var + LN_EPS) * g + b


def _nt_dot(a, b):
    return lax.dot_general(a, b, (((1,), (1,)), ((), ())), preferred_element_type=F32)


def _mods_kernel(c_ref, w_ref, b_ref, o_ref):
    c = c_ref[...]
    sc = c * jax.nn.sigmoid(c)
    o_ref[...] = jnp.dot(sc, w_ref[...], preferred_element_type=F32,
                         precision=lax.Precision.HIGHEST) + b_ref[...]


def _mods(c, w_ada, b_ada):
    bsz, d = c.shape
    n = w_ada.shape[1]
    tn = 1024
    return pl.pallas_call(
        _mods_kernel,
        grid=(n // tn,),
        in_specs=[pl.BlockSpec((bsz, d), lambda j: (0, 0)),
                  pl.BlockSpec((d, tn), lambda j: (0, j)),
                  pl.BlockSpec((1, tn), lambda j: (0, j))],
        out_specs=pl.BlockSpec((bsz, tn), lambda j: (0, j)),
        out_shape=jax.ShapeDtypeStruct((bsz, n), F32),
        compiler_params=pltpu.CompilerParams(dimension_semantics=("arbitrary",),
                                             vmem_limit_bytes=VMEM_LIMIT),
        name="adaln_mods",
    )(c, w_ada, b_ada.reshape(1, n))


def _ffn_kernel(x_ref, mods_ref, wg_ref, wu_ref, wd_ref, lng_ref, lnb_ref, o_ref,
                h_ref, a_ref, *, mod_base):
    x = x_ref[...]
    shift = mods_ref[mod_base:mod_base + 1, :]
    scale = mods_ref[mod_base + 1:mod_base + 2, :]
    gate = mods_ref[mod_base + 2:mod_base + 3, :]
    h_ref[...] = (x * (1.0 + scale) + shift).astype(BF16)
    d_ff = wg_ref.shape[1]
    for c0 in range(0, d_ff, FFN_TF):
        h = h_ref[...]
        g = jnp.dot(h, wg_ref[:, c0:c0 + FFN_TF], preferred_element_type=F32)
        u = jnp.dot(h, wu_ref[:, c0:c0 + FFN_TF], preferred_element_type=F32)
        a_ref[:, c0:c0 + FFN_TF] = (g * jax.nn.sigmoid(g) * u).astype(BF16)
    y = jnp.dot(a_ref[...], wd_ref[...], preferred_element_type=F32)
    z = ALPHA * x + (1.0 + gate) * (MACARON * y)
    o_ref[...] = _layer_norm(z, lng_ref[...], lnb_ref[...])


def _ffn(x, mods, wg, wu, wd, ln_g, ln_b, mod_base):
    bsz, seq, d = x.shape
    d_ff = wg.shape[1]
    const = lambda b, i: (0, 0)
    return pl.pallas_call(
        functools.partial(_ffn_kernel, mod_base=mod_base),
        grid=(bsz, seq // FFN_TM),
        in_specs=[pl.BlockSpec((None, FFN_TM, d), lambda b, i: (b, i, 0)),
                  pl.BlockSpec((None, N_MODS, d), lambda b, i: (b, 0, 0)),
                  pl.BlockSpec((d, d_ff), const, pipeline_mode=pl.Buffered(1)),
                  pl.BlockSpec((d, d_ff), const, pipeline_mode=pl.Buffered(1)),
                  pl.BlockSpec((d_ff, d), const, pipeline_mode=pl.Buffered(1)),
                  pl.BlockSpec((1, d), const),
                  pl.BlockSpec((1, d), const)],
        out_specs=pl.BlockSpec((None, FFN_TM, d), lambda b, i: (b, i, 0)),
        out_shape=jax.ShapeDtypeStruct((bsz, seq, d), F32),
        scratch_shapes=[pltpu.VMEM((FFN_TM, d), BF16),
                        pltpu.VMEM((FFN_TM, d_ff), BF16)],
        compiler_params=pltpu.CompilerParams(dimension_semantics=("arbitrary", "arbitrary"),
                                             vmem_limit_bytes=VMEM_LIMIT),
        name=f"ffn_mod{mod_base}",
    )(x, mods, wg, wu, wd, ln_g.reshape(1, d), ln_b.reshape(1, d))


def _dup_halves(t):
    lane = lax.broadcasted_iota(jnp.int32, t.shape, 1)
    swapped = pltpu.roll(t, HEAD_DIM, axis=1)
    low = lane < HEAD_DIM
    return jnp.where(low, t, swapped), jnp.where(low, swapped, t)


def _inproj_kernel(x_ref, mods_ref, w_ref, wf_ref, bf_ref,
                   qa_ref, ka_ref, va_ref, qb_ref, kb_ref, vb_ref, ccol_ref, crow_ref,
                   carry_ref):
    @pl.when(pl.program_id(1) == 0)
    def _():
        carry_ref[...] = jnp.zeros_like(carry_ref)

    x = x_ref[...]
    shift = mods_ref[3:4, :]
    scale = mods_ref[4:5, :]
    h = (x * (1.0 + scale) + shift).astype(BF16)

    o = 0
    qa_ref[...] = (jnp.dot(h, w_ref[:, o:o + D_SWA], preferred_element_type=F32)
                   * QK_SCALE).astype(BF16)
    o += D_SWA
    k0, k1 = _dup_halves(jnp.dot(h, w_ref[:, o:o + D_SWA_KV], preferred_element_type=F32))
    ka_ref[:, 0:LANES] = k0.astype(BF16)
    ka_ref[:, LANES:2 * LANES] = k1.astype(BF16)
    o += D_SWA_KV
    v0, v1 = _dup_halves(jnp.dot(h, w_ref[:, o:o + D_SWA_KV], preferred_element_type=F32))
    va_ref[:, 0:LANES] = v0.astype(BF16)
    va_ref[:, LANES:2 * LANES] = v1.astype(BF16)
    o += D_SWA_KV
    qb_ref[...] = (jnp.dot(h, w_ref[:, o:o + D_FOX], preferred_element_type=F32)
                   * QK_SCALE).astype(BF16)
    o += D_FOX
    kb_ref[...] = jnp.dot(h, w_ref[:, o:o + D_FOX], preferred_element_type=F32).astype(BF16)
    o += D_FOX
    vb_ref[...] = jnp.dot(h, w_ref[:, o:o + D_FOX], preferred_element_type=F32).astype(BF16)

    f = jnp.dot(h, wf_ref[...], preferred_element_type=F32) + bf_ref[...]
    cum = jnp.minimum(f, 0.0) - jnp.log1p(jnp.exp(-jnp.abs(f)))
    tm = cum.shape[0]
    row = lax.broadcasted_iota(jnp.int32, cum.shape, 0)
    k = 1
    while k < tm:
        cum = cum + jnp.where(row >= k, pltpu.roll(cum, k, axis=0), 0.0)
        k *= 2
    cum = cum + carry_ref[...]
    carry_ref[...] = cum[tm - 1:tm, :]
    ccol_ref[...] = cum
    cum_t = cum.T
    for r in range(tm // ATT_TK):
        crow_ref[r] = cum_t[0:FOX_HEADS, r * ATT_TK:(r + 1) * ATT_TK]


def _inproj(x1, mods, w_main, w_f, b_f):
    bsz, seq, d = x1.shape
    tm = PROJ_TM
    n_main = w_main.shape[1]
    const = lambda b, i: (0, 0)
    row_blk = lambda width: pl.BlockSpec((None, tm, width), lambda b, i: (b, i, 0))
    outs = pl.pallas_call(
        _inproj_kernel,
        grid=(bsz, seq // tm),
        in_specs=[row_blk(d),
                  pl.BlockSpec((None, N_MODS, d), lambda b, i: (b, 0, 0)),
                  pl.BlockSpec((d, n_main), const, pipeline_mode=pl.Buffered(1)),
                  pl.BlockSpec((d, LANES), const),
                  pl.BlockSpec((1, LANES), const)],
        out_specs=[row_blk(D_SWA), row_blk(2 * LANES), row_blk(2 * LANES),
                   row_blk(D_FOX), row_blk(D_FOX), row_blk(D_FOX), row_blk(LANES),
                   pl.BlockSpec((None, tm // ATT_TK, FOX_HEADS, ATT_TK),
                                lambda b, i: (b, i, 0, 0))],
        out_shape=[jax.ShapeDtypeStruct((bsz, seq, D_SWA), BF16),
                   jax.ShapeDtypeStruct((bsz, seq, 2 * LANES), BF16),
                   jax.ShapeDtypeStruct((bsz, seq, 2 * LANES), BF16),
                   jax.ShapeDtypeStruct((bsz, seq, D_FOX), BF16),
                   jax.ShapeDtypeStruct((bsz, seq, D_FOX), BF16),
                   jax.ShapeDtypeStruct((bsz, seq, D_FOX), BF16),
                   jax.ShapeDtypeStruct((bsz, seq, LANES), F32),
                   jax.ShapeDtypeStruct((bsz, seq // ATT_TK, FOX_HEADS, ATT_TK), F32)],
        scratch_shapes=[pltpu.VMEM((1, LANES), F32)],
        compiler_params=pltpu.CompilerParams(dimension_semantics=("arbitrary", "arbitrary"),
                                             vmem_limit_bytes=VMEM_LIMIT),
        name="in_proj",
    )(x1, mods, w_main, w_f, b_f)
    return outs


def _half_mask(shape, e):
    lane = lax.broadcasted_iota(jnp.int32, shape, 1)
    return (lane < HEAD_DIM) if e == 0 else (lane >= HEAD_DIM)


def _attn_kernel(sinks_ref, x1_ref, mods_ref, qa_ref, kac_ref, kap_ref, vac_ref, vap_ref,
                 qb_ref, kb_ref, vb_ref, ccol_ref, crow_ref, gain_ref, wout_ref,
                 lng_ref, lnb_ref, o_ref,
                 qm_sc, m_sc, l_sc, acc_sc, o_sc):
    i = pl.program_id(1)
    tq, tk = ATT_TQ, ATT_TK

    qpos = lax.broadcasted_iota(jnp.int32, (BLOCK, 2 * BLOCK), 0)
    kidx = lax.broadcasted_iota(jnp.int32, (BLOCK, 2 * BLOCK), 1)
    dist = qpos - (kidx - BLOCK)
    window = (dist >= 0) & (dist < WINDOW)
    first_valid = jnp.where(i == 0, BLOCK, 0)
    dist_f = dist.astype(F32)
    for nb in range(tq // BLOCK):
        r0 = nb * BLOCK
        if nb == 0:
            k_prev, v_prev = kap_ref[...], vap_ref[...]
            mask = window & (kidx >= first_valid)
        else:
            k_prev, v_prev = kac_ref[r0 - BLOCK:r0, :], vac_ref[r0 - BLOCK:r0, :]
            mask = window
        kk = jnp.concatenate([k_prev, kac_ref[r0:r0 + BLOCK, :]], axis=0)
        vv = jnp.concatenate([v_prev, vac_ref[r0:r0 + BLOCK, :]], axis=0)
        for hh in range(SWA_Q_HEADS):
            g, pair, e = hh // SWA_GROUP, hh // 2, hh % 2
            qc = qa_ref[r0:r0 + BLOCK, pair * LANES:(pair + 1) * LANES]
            qm = jnp.where(_half_mask(qc.shape, e), qc, jnp.zeros_like(qc))
            s = _nt_dot(qm, kk[:, g * LANES:(g + 1) * LANES])
            slope = 2.0 ** (-8.0 * (hh + 1) / SWA_Q_HEADS)
            s = jnp.where(mask, s - slope * dist_f, NEG_BIG)
            sink = sinks_ref[hh]
            m = jnp.maximum(jnp.max(s, axis=-1, keepdims=True), sink)
            p = jnp.exp(s - m)
            denom = jnp.sum(p, axis=-1, keepdims=True) + jnp.exp(sink - m)
            pv = jnp.dot(p.astype(BF16), vv[:, g * LANES:(g + 1) * LANES],
                         preferred_element_type=F32)
            oh = pv / denom
            if e == 0:
                o_even = oh
            else:
                o_sc[r0:r0 + BLOCK, pair * LANES:(pair + 1) * LANES] = jnp.where(
                    _half_mask(oh.shape, 0), o_even, oh)

    for hh in range(FOX_HEADS):
        pair, e = hh // 2, hh % 2
        qc = qb_ref[:, pair * LANES:(pair + 1) * LANES]
        qm_sc[hh] = jnp.where(_half_mask(qc.shape, e), qc, jnp.zeros_like(qc))
    m_sc[...] = jnp.full(m_sc.shape, NEG_BIG, F32)
    l_sc[...] = jnp.zeros(l_sc.shape, F32)
    acc_sc[...] = jnp.zeros(acc_sc.shape, F32)

    ccol = ccol_ref[...]

    def fox_block(j, causal):
        k0 = pl.multiple_of(j * tk, tk)
        kblk = kb_ref[pl.ds(k0, tk), :]
        vblk = vb_ref[pl.ds(k0, tk), :]
        crow = crow_ref[j]
        for hh in range(FOX_HEADS):
            pair = hh // 2
            s = _nt_dot(qm_sc[hh], kblk[:, pair * LANES:(pair + 1) * LANES])
            s = s + ccol[:, hh:hh + 1] - crow[hh:hh + 1, :]
            if causal is not None:
                s = jnp.where(causal, s, NEG_BIG)
            m_prev = m_sc[hh]
            m_new = jnp.maximum(m_prev, jnp.max(s, axis=-1, keepdims=True))
            a = jnp.exp(m_prev - m_new)
            p = jnp.exp(s - m_new)
            l_sc[hh] = a * l_sc[hh] + jnp.sum(p, axis=-1, keepdims=True)
            acc_sc[hh] = a * acc_sc[hh] + jnp.dot(
                p.astype(BF16), vblk[:, pair * LANES:(pair + 1) * LANES],
                preferred_element_type=F32)
            m_sc[hh] = m_new

    def body(j, carry):
        fox_block(j, None)
        return carry

    lax.fori_loop(0, i, body, 0)
    rr = lax.broadcasted_iota(jnp.int32, (tq, tk), 0)
    cc = lax.broadcasted_iota(jnp.int32, (tq, tk), 1)
    fox_block(i, cc <= rr)

    for pair in range(FOX_HEADS // 2):
        o0 = acc_sc[2 * pair] / l_sc[2 * pair]
        o1 = acc_sc[2 * pair + 1] / l_sc[2 * pair + 1]
        o_sc[:, D_SWA + pair * LANES:D_SWA + (pair + 1) * LANES] = jnp.where(
            _half_mask(o0.shape, 0), o0, o1)

    oa = o_sc[:, 0:D_SWA]
    ob = o_sc[:, D_SWA:D_SWA + D_FOX]
    oa = oa * lax.rsqrt(jnp.mean(oa * oa, axis=-1, keepdims=True) + RMS_EPS) * gain_ref[:, 0:D_SWA]
    ob = ob * lax.rsqrt(jnp.mean(ob * ob, axis=-1, keepdims=True) + RMS_EPS) * gain_ref[:, D_SWA:]
    on = jnp.concatenate([oa, ob], axis=-1).astype(BF16)
    y = jnp.dot(on, wout_ref[...], preferred_element_type=F32)
    gate = mods_ref[5:6, :]
    z = ALPHA * x1_ref[...] + (1.0 + gate) * y
    o_ref[...] = _layer_norm(z, lng_ref[...], lnb_ref[...])


def _attention(x1, mods, qa, ka, va, qb, kb, vb, ccol, crow, sinks, gain, w_out, ln_g, ln_b):
    bsz, seq, d = x1.shape
    tq = ATT_TQ
    blocks_per_tile = tq // BLOCK
    const = lambda b, i, s: (0, 0)
    tile = lambda width: pl.BlockSpec((None, tq, width), lambda b, i, s: (b, i, 0))
    prev = pl.BlockSpec((None, BLOCK, 2 * LANES),
                        lambda b, i, s: (b, jnp.maximum(i * blocks_per_tile - 1, 0), 0))
    full = lambda width: pl.BlockSpec((None, seq, width), lambda b, i, s: (b, 0, 0))
    grid_spec = pltpu.PrefetchScalarGridSpec(
        num_scalar_prefetch=1,
        grid=(bsz, seq // tq),
        in_specs=[tile(d),
                  pl.BlockSpec((None, N_MODS, d), lambda b, i, s: (b, 0, 0)),
                  tile(D_SWA), tile(2 * LANES), prev, tile(2 * LANES), prev,
                  tile(D_FOX), full(D_FOX), full(D_FOX),
                  tile(LANES),
                  pl.BlockSpec((None, seq // ATT_TK, FOX_HEADS, ATT_TK),
                               lambda b, i, s: (b, 0, 0, 0)),
                  pl.BlockSpec((1, d), const),
                  pl.BlockSpec((d, d), const),
                  pl.BlockSpec((1, d), const),
                  pl.BlockSpec((1, d), const)],
        out_specs=tile(d),
        scratch_shapes=[pltpu.VMEM((FOX_HEADS, tq, LANES), BF16),
                        pltpu.VMEM((FOX_HEADS, tq, 1), F32),
                        pltpu.VMEM((FOX_HEADS, tq, 1), F32),
                        pltpu.VMEM((FOX_HEADS, tq, LANES), F32),
                        pltpu.VMEM((tq, d), F32)],
    )
    return pl.pallas_call(
        _attn_kernel,
        grid_spec=grid_spec,
        out_shape=jax.ShapeDtypeStruct((bsz, seq, d), F32),
        compiler_params=pltpu.CompilerParams(dimension_semantics=("arbitrary", "arbitrary"),
                                             vmem_limit_bytes=VMEM_LIMIT),
        name="token_mix",
    )(sinks, x1, mods, qa, ka, ka, va, va, qb, kb, vb, ccol, crow,
      gain.reshape(1, d), w_out, ln_g.reshape(1, d), ln_b.reshape(1, d))


def kernel(x, c, w_ada, b_ada, ffn1_w_gate, ffn1_w_up, ffn1_w_down, w_in, b_forget,
           swa_sinks, grp_gain, w_out, ffn2_w_gate, ffn2_w_up, ffn2_w_down,
           ln1_g, ln1_b, ln2_g, ln2_b, ln3_g, ln3_b):
    bsz, seq, d = x.shape
    assert w_ada.shape[0] == DEPTH == 1
    assert seq % FFN_TM == 0 and seq % PROJ_TM == 0 and seq % ATT_TQ == 0
    assert PROJ_TM % ATT_TK == 0 and ATT_TQ == ATT_TK and ATT_TQ % BLOCK == 0
    l = 0
    mods = _mods(c, w_ada[l], b_ada[l]).reshape(bsz, N_MODS, d)

    x1 = _ffn(x, mods, ffn1_w_gate[l].astype(BF16), ffn1_w_up[l].astype(BF16),
              ffn1_w_down[l].astype(BF16), ln1_g[l], ln1_b[l], mod_base=0)

    n_main = D_SWA + 2 * D_SWA_KV + 3 * D_FOX
    w_main = w_in[l][:, :n_main].astype(BF16)
    w_f = jnp.pad(w_in[l][:, n_main:], ((0, 0), (0, LANES - FOX_HEADS))).astype(BF16)
    b_f = jnp.pad(b_forget[l], (0, LANES - FOX_HEADS)).reshape(1, LANES)
    qa, ka, va, qb, kb, vb, ccol, crow = _inproj(x1, mods, w_main, w_f, b_f)

    x2 = _attention(x1, mods, qa, ka, va, qb, kb, vb, ccol, crow, swa_sinks[l],
                    grp_gain[l], w_out[l].astype(BF16), ln2_g[l], ln2_b[l])

    return _ffn(x2, mods, ffn2_w_gate[l].astype(BF16), ffn2_w_up[l].astype(BF16),
                ffn2_w_down[l].astype(BF16), ln3_g[l], ln3_b[l], mod_base=6)
```

```python
import functools
import math

import numpy as np
import jax
import jax.numpy as jnp
from jax import lax
from jax.experimental import pallas as pl
from jax.experimental.pallas import tpu as pltpu

F32 = jnp.float32
BF16 = jnp.bfloat16

SWA_Q_HEADS = 8
SWA_KV_HEADS = 2
SWA_GROUP = SWA_Q_HEADS // SWA_KV_HEADS
FOX_HEADS = 8
HEAD_DIM = 64
D_SWA = SWA_Q_HEADS * HEAD_DIM
D_SWA_KV = SWA_KV_HEADS * HEAD_DIM
D_FOX = FOX_HEADS * HEAD_DIM
WINDOW = 128
BLOCK = 128
MACARON = 0.5
DEPTH = 1
ALPHA = (2.0 * DEPTH) ** 0.25
LN_EPS = 1e-5
RMS_EPS = 1e-6
N_MODS = 9
QK_SCALE = 1.0 / math.sqrt(HEAD_DIM)

LANES = 128
SUBLANES = 8
VMEM_LIMIT = 56 * 1024 * 1024

NEG_BIG = -1e30

FFN_TM = 512
FFN_TF = 256
PROJ_TM = 512
ATT_TQ = 256
ATT_TK = 256
FOX_LOOKAHEAD = 4

SLOT = LANES
N_SPLIT = 3
Q_GATE = HEAD_DIM
K_GATE = HEAD_DIM + N_SPLIT
D_FOX_SLOTS = FOX_HEADS * SLOT

OFF_QA = 0
OFF_KA = OFF_QA + D_SWA
OFF_VA = OFF_KA + D_SWA_KV
OFF_QB = OFF_VA + D_SWA_KV
OFF_KB = OFF_QB + D_FOX
OFF_VB = OFF_KB + D_FOX
N_MAIN = OFF_VB + D_FOX


def _layer_norm(z, g, b):
    mu = jnp.mean(z, axis=-1, keepdims=True)
    zc = z - mu
    var = jnp.mean(zc * zc, axis=-1, keepdims=True)
    return zc * lax.rsqrt(var + LN_EPS) * g + b


def _nt_dot(a, b):
    return lax.dot_general(a, b, (((1,), (1,)), ((), ())), preferred_element_type=F32)


def _split3(x):
    hi = x.astype(BF16).astype(F32)
    r = x - hi
    mid = r.astype(BF16).astype(F32)
    lo = (r - mid).astype(BF16).astype(F32)
    return hi, mid, lo


def _mods_kernel(c_ref, w_ref, b_ref, o_ref):
    c = c_ref[...]
    sc = c * jax.nn.sigmoid(c)
    o_ref[...] = jnp.dot(sc, w_ref[...], preferred_element_type=F32,
                         precision=lax.Precision.HIGHEST) + b_ref[...]


def _mods(c, w_ada, b_ada):
    bsz, d = c.shape
    n = w_ada.shape[1]
    tn = 1024
    return pl.pallas_call(
        _mods_kernel,
        grid=(n // tn,),
        in_specs=[pl.BlockSpec((bsz, d), lambda j: (0, 0)),
                  pl.BlockSpec((d, tn), lambda j: (0, j)),
                  pl.BlockSpec((1, tn), lambda j: (0, j))],
        out_specs=pl.BlockSpec((bsz, tn), lambda j: (0, j)),
        out_shape=jax.ShapeDtypeStruct((bsz, n), F32),
        compiler_params=pltpu.CompilerParams(dimension_semantics=("arbitrary",),
                                             vmem_limit_bytes=VMEM_LIMIT),
        name="adaln_mods",
    )(c, w_ada, b_ada.reshape(1, n))


def _ffn_kernel(x_ref, mods_ref, wg_ref, wu_ref, wd_ref, lng_ref, lnb_ref, o_ref,
                h_ref, a_ref, *, mod_base):
    x = x_ref[...]
    shift = mods_ref[mod_base:mod_base + 1, :]
    scale = mods_ref[mod_base + 1:mod_base + 2, :]
    gate = mods_ref[mod_base + 2:mod_base + 3, :]
    h_ref[...] = (x * (1.0 + scale) + shift).astype(BF16)
    d_ff = wg_ref.shape[1]
    for c0 in range(0, d_ff, FFN_TF):
        h = h_ref[...]
        g = jnp.dot(h, wg_ref[:, c0:c0 + FFN_TF], preferred_element_type=F32)
        u = jnp.dot(h, wu_ref[:, c0:c0 + FFN_TF], preferred_element_type=F32)
        a_ref[:, c0:c0 + FFN_TF] = (g * jax.nn.sigmoid(g) * u).astype(BF16)
    y = jnp.dot(a_ref[...], wd_ref[...], preferred_element_type=F32)
    z = ALPHA * x + (1.0 + gate) * (MACARON * y)
    o_ref[...] = _layer_norm(z, lng_ref[...], lnb_ref[...])


def _ffn(x, mods, wg, wu, wd, ln_g, ln_b, mod_base):
    bsz, seq, d = x.shape
    d_ff = wg.shape[1]
    const = lambda b, i: (0, 0)
    return pl.pallas_call(
        functools.partial(_ffn_kernel, mod_base=mod_base),
        grid=(bsz, seq // FFN_TM),
        in_specs=[pl.BlockSpec((None, FFN_TM, d), lambda b, i: (b, i, 0)),
                  pl.BlockSpec((None, N_MODS, d), lambda b, i: (b, 0, 0)),
                  pl.BlockSpec((d, d_ff), const, pipeline_mode=pl.Buffered(1)),
                  pl.BlockSpec((d, d_ff), const, pipeline_mode=pl.Buffered(1)),
                  pl.BlockSpec((d_ff, d), const, pipeline_mode=pl.Buffered(1)),
                  pl.BlockSpec((1, d), const),
                  pl.BlockSpec((1, d), const)],
        out_specs=pl.BlockSpec((None, FFN_TM, d), lambda b, i: (b, i, 0)),
        out_shape=jax.ShapeDtypeStruct((bsz, seq, d), F32),
        scratch_shapes=[pltpu.VMEM((FFN_TM, d), BF16),
                        pltpu.VMEM((FFN_TM, d_ff), BF16)],
        compiler_params=pltpu.CompilerParams(dimension_semantics=("arbitrary", "arbitrary"),
                                             vmem_limit_bytes=VMEM_LIMIT),
        name=f"ffn_mod{mod_base}",
    )(x, mods, wg, wu, wd, ln_g.reshape(1, d), ln_b.reshape(1, d))


def _gate_placement():
    place = np.zeros((N_SPLIT * LANES, D_FOX_SLOTS), np.float32)
    ones = np.zeros((1, D_FOX_SLOTS), np.float32)
    for h in range(FOX_HEADS):
        for t in range(N_SPLIT):
            place[t * LANES + h, h * SLOT + K_GATE + t] = 1.0
            ones[0, h * SLOT + Q_GATE + t] = 1.0
    return jnp.asarray(place, BF16), jnp.asarray(ones, F32)


def _inproj_kernel(x_ref, mods_ref, w_ref, wt_ref, wf_ref, bf_ref, place_ref, ones_ref,
                   qat_ref, ka_ref, vat_ref, qbt_ref, kb_ref, vbt_ref,
                   carry_ref):
    @pl.when(pl.program_id(1) == 0)
    def _():
        carry_ref[...] = jnp.zeros_like(carry_ref)

    x = x_ref[...]
    tm = x.shape[0]
    shift = mods_ref[3:4, :]
    scale = mods_ref[4:5, :]
    h = (x * (1.0 + scale) + shift).astype(BF16)

    f = jnp.dot(h, wf_ref[...], preferred_element_type=F32) + bf_ref[...]
    cum = jnp.minimum(f, 0.0) - jnp.log1p(jnp.exp(-jnp.abs(f)))
    row = lax.broadcasted_iota(jnp.int32, cum.shape, 0)
    k = 1
    while k < tm:
        cum = cum + jnp.where(row >= k, pltpu.roll(cum, k, axis=0), 0.0)
        k *= 2
    cum = cum + carry_ref[...]
    carry_ref[...] = cum[tm - 1:tm, :]
    cum_t = cum.T[0:SUBLANES, :]

    qat_ref[...] = (_nt_dot(wt_ref[OFF_QA:OFF_QA + D_SWA, :], h) * QK_SCALE).astype(BF16)
    ka_ref[...] = jnp.dot(h, w_ref[:, OFF_KA:OFF_KA + D_SWA_KV],
                          preferred_element_type=F32).astype(BF16)
    vat_ref[...] = _nt_dot(wt_ref[OFF_VA:OFF_VA + D_SWA_KV, :], h).astype(BF16)

    qbt = _nt_dot(wt_ref[OFF_QB:OFF_QB + D_FOX, :], h) * QK_SCALE
    q_hi, q_mid, q_lo = _split3(cum_t)
    sub = lax.broadcasted_iota(jnp.int32, (SUBLANES, tm), 0)
    for hh in range(FOX_HEADS):
        extra = jnp.where(sub == 0, q_hi[hh:hh + 1, :],
                jnp.where(sub == 1, q_mid[hh:hh + 1, :],
                jnp.where(sub == 2, q_lo[hh:hh + 1, :],
                jnp.where(sub < 2 * N_SPLIT, 1.0, 0.0))))
        tail = jnp.concatenate(
            [extra, jnp.zeros((SLOT - HEAD_DIM - SUBLANES, tm), F32)], axis=0)
        qbt_ref[hh * SLOT:hh * SLOT + HEAD_DIM, :] = (
            qbt[hh * HEAD_DIM:(hh + 1) * HEAD_DIM, :].astype(BF16))
        qbt_ref[hh * SLOT + HEAD_DIM:(hh + 1) * SLOT, :] = tail.astype(BF16)

    kb = jnp.dot(h, w_ref[:, OFF_KB:OFF_KB + D_FOX], preferred_element_type=F32)
    k_hi, k_mid, k_lo = _split3(-cum)
    parts = jnp.concatenate([k_hi, k_mid, k_lo], axis=-1).astype(BF16)
    gates = jnp.dot(parts, place_ref[...], preferred_element_type=F32) + ones_ref[...]
    low = lax.broadcasted_iota(jnp.int32, (tm, LANES), 1) < HEAD_DIM
    for pair in range(FOX_HEADS // 2):
        kp = kb[:, pair * LANES:(pair + 1) * LANES]
        odd = pltpu.roll(kp, HEAD_DIM, axis=1)
        for e, src in ((0, kp), (1, odd)):
            hh = 2 * pair + e
            kb_ref[:, hh * SLOT:(hh + 1) * SLOT] = jnp.where(
                low, src, gates[:, hh * SLOT:(hh + 1) * SLOT]).astype(BF16)

    vbt = _nt_dot(wt_ref[OFF_VB:OFF_VB + D_FOX, :], h).astype(BF16)
    for r in range(tm // ATT_TK):
        vbt_ref[r] = vbt[:, r * ATT_TK:(r + 1) * ATT_TK]


def _inproj(x1, mods, w_main, w_main_t, w_f, b_f):
    bsz, seq, d = x1.shape
    tm = PROJ_TM
    place, ones = _gate_placement()
    const = lambda b, i: (0, 0)
    whole = lambda a: pl.BlockSpec(a.shape, const, pipeline_mode=pl.Buffered(1))
    rows = lambda width: pl.BlockSpec((None, tm, width), lambda b, i: (b, i, 0))
    cols = lambda height: pl.BlockSpec((None, height, tm), lambda b, i: (b, 0, i))
    return pl.pallas_call(
        _inproj_kernel,
        grid=(bsz, seq // tm),
        in_specs=[rows(d),
                  pl.BlockSpec((None, N_MODS, d), lambda b, i: (b, 0, 0)),
                  whole(w_main), whole(w_main_t), whole(w_f), whole(b_f),
                  whole(place), whole(ones)],
        out_specs=[cols(D_SWA), rows(D_SWA_KV), cols(D_SWA_KV),
                   cols(D_FOX_SLOTS), rows(D_FOX_SLOTS),
                   pl.BlockSpec((None, tm // ATT_TK, D_FOX, ATT_TK), lambda b, i: (b, i, 0, 0))],
        out_shape=[jax.ShapeDtypeStruct((bsz, D_SWA, seq), BF16),
                   jax.ShapeDtypeStruct((bsz, seq, D_SWA_KV), BF16),
                   jax.ShapeDtypeStruct((bsz, D_SWA_KV, seq), BF16),
                   jax.ShapeDtypeStruct((bsz, D_FOX_SLOTS, seq), BF16),
                   jax.ShapeDtypeStruct((bsz, seq, D_FOX_SLOTS), BF16),
                   jax.ShapeDtypeStruct((bsz, seq // ATT_TK, D_FOX, ATT_TK), BF16)],
        scratch_shapes=[pltpu.VMEM((1, LANES), F32)],
        compiler_params=pltpu.CompilerParams(dimension_semantics=("arbitrary", "arbitrary"),
                                             vmem_limit_bytes=VMEM_LIMIT),
        name="in_proj",
    )(x1, mods, w_main, w_main_t, w_f, b_f, place, ones)


def _attn_kernel(sinks_ref, x1_ref, mods_ref, qat_ref, kac_ref, kap_ref, vatc_ref, vatp_ref,
                 qbt_ref, kb_ref, vbt_ref, gain_ref, wout_ref, lng_ref, lnb_ref, o_ref,
                 m_sc, l_sc, acc_sc, ot_sc):
    i = pl.program_id(1)
    tq, tk = ATT_TQ, ATT_TK

    kidx = lax.broadcasted_iota(jnp.int32, (2 * BLOCK, BLOCK), 0)
    qidx = lax.broadcasted_iota(jnp.int32, (2 * BLOCK, BLOCK), 1)
    dist = qidx - (kidx - BLOCK)
    window = (dist >= 0) & (dist < WINDOW)
    first_valid = jnp.where(i == 0, BLOCK, 0)
    dist_f = dist.astype(F32)
    zeros_q = jnp.zeros((HEAD_DIM, BLOCK), BF16)
    for nb in range(tq // BLOCK):
        r0 = nb * BLOCK
        if nb == 0:
            k_prev, v_prev = kap_ref[...], vatp_ref[...]
            mask = window & (kidx >= first_valid)
        else:
            k_prev, v_prev = kac_ref[r0 - BLOCK:r0, :], vatc_ref[:, r0 - BLOCK:r0]
            mask = window
        kk = jnp.concatenate([k_prev, kac_ref[r0:r0 + BLOCK, :]], axis=0)
        vv = jnp.concatenate([v_prev, vatc_ref[:, r0:r0 + BLOCK]], axis=1)
        for hh in range(SWA_Q_HEADS):
            g = hh // SWA_GROUP
            qh = qat_ref[hh * HEAD_DIM:(hh + 1) * HEAD_DIM, r0:r0 + BLOCK]
            qop = jnp.concatenate([qh, zeros_q] if g == 0 else [zeros_q, qh], axis=0)
            s = jnp.dot(kk, qop, preferred_element_type=F32)
            slope = 2.0 ** (-8.0 * (hh + 1) / SWA_Q_HEADS)
            s = jnp.where(mask, s - slope * dist_f, NEG_BIG)
            sink = sinks_ref[hh]
            m = jnp.maximum(jnp.max(s, axis=0, keepdims=True), sink)
            p = jnp.exp(s - m)
            denom = jnp.sum(p, axis=0, keepdims=True) + jnp.exp(sink - m)
            pv = jnp.dot(vv[g * HEAD_DIM:(g + 1) * HEAD_DIM, :], p.astype(BF16),
                         preferred_element_type=F32)
            ot_sc[hh * HEAD_DIM:(hh + 1) * HEAD_DIM, r0:r0 + BLOCK] = pv / denom

    m_sc[...] = jnp.full(m_sc.shape, NEG_BIG, F32)
    l_sc[...] = jnp.zeros(l_sc.shape, F32)
    acc_sc[...] = jnp.zeros(acc_sc.shape, F32)

    def fox_block(j, causal):
        def scores(hh):
            return jnp.dot(kb_ref[j, :, hh * SLOT:(hh + 1) * SLOT],
                           qbt_ref[hh * SLOT:(hh + 1) * SLOT, :],
                           preferred_element_type=F32)

        pending = [scores(hh) for hh in range(FOX_LOOKAHEAD)]
        for hh in range(FOX_HEADS):
            if hh + FOX_LOOKAHEAD < FOX_HEADS:
                pending.append(scores(hh + FOX_LOOKAHEAD))
            s = pending[hh]
            if causal is not None:
                s = jnp.where(causal, s, NEG_BIG)
            m_prev = m_sc[hh]
            m_new = jnp.maximum(m_prev, jnp.max(s, axis=0, keepdims=True))
            a = jnp.exp(m_prev - m_new)
            p = jnp.exp(s - m_new)
            l_sc[hh] = a * l_sc[hh] + jnp.sum(p, axis=0, keepdims=True)
            acc_sc[hh] = a * acc_sc[hh] + jnp.dot(
                vbt_ref[j, hh * HEAD_DIM:(hh + 1) * HEAD_DIM, :], p.astype(BF16),
                preferred_element_type=F32)
            m_sc[hh] = m_new

    def body(j, carry):
        fox_block(j, None)
        return carry

    lax.fori_loop(0, i, body, 0)
    kk_i = lax.broadcasted_iota(jnp.int32, (tk, tq), 0)
    qq_i = lax.broadcasted_iota(jnp.int32, (tk, tq), 1)
    fox_block(i, kk_i <= qq_i)

    for hh in range(FOX_HEADS):
        ot_sc[D_SWA + hh * HEAD_DIM:D_SWA + (hh + 1) * HEAD_DIM, :] = (
            acc_sc[hh] / l_sc[hh])

    o = ot_sc[...].T
    oa = o[:, 0:D_SWA]
    ob = o[:, D_SWA:D_SWA + D_FOX]
    oa = oa * lax.rsqrt(jnp.mean(oa * oa, axis=-1, keepdims=True) + RMS_EPS) * gain_ref[:, 0:D_SWA]
    ob = ob * lax.rsqrt(jnp.mean(ob * ob, axis=-1, keepdims=True) + RMS_EPS) * gain_ref[:, D_SWA:]
    on = jnp.concatenate([oa, ob], axis=-1).astype(BF16)
    y = jnp.dot(on, wout_ref[...], preferred_element_type=F32)
    gate = mods_ref[5:6, :]
    z = ALPHA * x1_ref[...] + (1.0 + gate) * y
    o_ref[...] = _layer_norm(z, lng_ref[...], lnb_ref[...])


def _attention(x1, mods, qat, ka, vat, qbt, kb, vbt, sinks, gain, w_out, ln_g, ln_b):
    bsz, seq, d = x1.shape
    tq = ATT_TQ
    blocks_per_tile = tq // BLOCK
    prev_blk = lambda i: jnp.maximum(i * blocks_per_tile - 1, 0)
    const = lambda b, i, s: (0, 0)
    kb_blocks = kb.reshape(bsz, seq // ATT_TK, ATT_TK, D_FOX_SLOTS)
    grid_spec = pltpu.PrefetchScalarGridSpec(
        num_scalar_prefetch=1,
        grid=(bsz, seq // tq),
        in_specs=[pl.BlockSpec((None, tq, d), lambda b, i, s: (b, i, 0)),
                  pl.BlockSpec((None, N_MODS, d), lambda b, i, s: (b, 0, 0)),
                  pl.BlockSpec((None, D_SWA, tq), lambda b, i, s: (b, 0, i)),
                  pl.BlockSpec((None, tq, D_SWA_KV), lambda b, i, s: (b, i, 0)),
                  pl.BlockSpec((None, BLOCK, D_SWA_KV), lambda b, i, s: (b, prev_blk(i), 0)),
                  pl.BlockSpec((None, D_SWA_KV, tq), lambda b, i, s: (b, 0, i)),
                  pl.BlockSpec((None, D_SWA_KV, BLOCK), lambda b, i, s: (b, 0, prev_blk(i))),
                  pl.BlockSpec((None, D_FOX_SLOTS, tq), lambda b, i, s: (b, 0, i)),
                  pl.BlockSpec((None, seq // ATT_TK, ATT_TK, D_FOX_SLOTS),
                               lambda b, i, s: (b, 0, 0, 0)),
                  pl.BlockSpec((None, seq // ATT_TK, D_FOX, ATT_TK),
                               lambda b, i, s: (b, 0, 0, 0)),
                  pl.BlockSpec((1, d), const),
                  pl.BlockSpec((d, d), const),
                  pl.BlockSpec((1, d), const),
                  pl.BlockSpec((1, d), const)],
        out_specs=pl.BlockSpec((None, tq, d), lambda b, i, s: (b, i, 0)),
        scratch_shapes=[pltpu.VMEM((FOX_HEADS, 1, tq), F32),
                        pltpu.VMEM((FOX_HEADS, 1, tq), F32),
                        pltpu.VMEM((FOX_HEADS, HEAD_DIM, tq), F32),
                        pltpu.VMEM((d, tq), F32)],
    )
    return pl.pallas_call(
        _attn_kernel,
        grid_spec=grid_spec,
        out_shape=jax.ShapeDtypeStruct((bsz, seq, d), F32),
        compiler_params=pltpu.CompilerParams(dimension_semantics=("arbitrary", "arbitrary"),
                                             vmem_limit_bytes=VMEM_LIMIT),
        name="token_mix",
    )(sinks, x1, mods, qat, ka, ka, vat, vat, qbt, kb_blocks, vbt,
      gain.reshape(1, d), w_out, ln_g.reshape(1, d), ln_b.reshape(1, d))


def kernel(x, c, w_ada, b_ada, ffn1_w_gate, ffn1_w_up, ffn1_w_down, w_in, b_forget,
           swa_sinks, grp_gain, w_out, ffn2_w_gate, ffn2_w_up, ffn2_w_down,
           ln1_g, ln1_b, ln2_g, ln2_b, ln3_g, ln3_b):
    bsz, seq, d = x.shape
    assert w_ada.shape[0] == DEPTH == 1
    assert seq % FFN_TM == 0 and seq % PROJ_TM == 0 and seq % ATT_TQ == 0
    assert PROJ_TM % ATT_TK == 0 and ATT_TQ == ATT_TK and ATT_TQ % BLOCK == 0
    assert w_in.shape[2] == N_MAIN + FOX_HEADS and D_SWA + D_FOX == d
    l = 0
    mods = _mods(c, w_ada[l], b_ada[l]).reshape(bsz, N_MODS, d)

    x1 = _ffn(x, mods, ffn1_w_gate[l].astype(BF16), ffn1_w_up[l].astype(BF16),
              ffn1_w_down[l].astype(BF16), ln1_g[l], ln1_b[l], mod_base=0)

    w_main = w_in[l][:, :N_MAIN].astype(BF16)
    w_f = jnp.pad(w_in[l][:, N_MAIN:], ((0, 0), (0, LANES - FOX_HEADS))).astype(BF16)
    b_f = jnp.pad(b_forget[l], (0, LANES - FOX_HEADS)).reshape(1, LANES)
    qat, ka, vat, qbt, kb, vbt = _inproj(x1, mods, w_main, w_main.T, w_f, b_f)

    x2 = _attention(x1, mods, qat, ka, vat, qbt, kb, vbt, swa_sinks[l],
                    grp_gain[l], w_out[l].astype(BF16), ln2_g[l], ln2_b[l])

    return _ffn(x2, mods, ffn2_w_gate[l].astype(BF16), ffn2_w_up[l].astype(BF16),
                ffn2_w_down[l].astype(BF16), ln3_g[l], ln3_b[l], mod_base=6)
```

```python
import functools
import math

import numpy as np
import jax
import jax.numpy as jnp
from jax import lax
from jax.experimental import pallas as pl
from jax.experimental.pallas import tpu as pltpu

F32 = jnp.float32
BF16 = jnp.bfloat16

SWA_Q_HEADS = 8
SWA_KV_HEADS = 2
SWA_GROUP = SWA_Q_HEADS // SWA_KV_HEADS
FOX_HEADS = 8
HEAD_DIM = 64
D_SWA = SWA_Q_HEADS * HEAD_DIM
D_SWA_KV = SWA_KV_HEADS * HEAD_DIM
D_FOX = FOX_HEADS * HEAD_DIM
WINDOW = 128
BLOCK = 128
MACARON = 0.5
DEPTH = 1
ALPHA = (2.0 * DEPTH) ** 0.25
LN_EPS = 1e-5
RMS_EPS = 1e-6
N_MODS = 9
QK_SCALE = 1.0 / math.sqrt(HEAD_DIM)

LANES = 128
SUBLANES = 8
VMEM_LIMIT = 56 * 1024 * 1024

NEG_BIG = -1e30

FFN_TM = 512
FFN_TF = 256
PROJ_TM = 512
ATT_TQ = 256
ATT_TK = 256
FOX_LOOKAHEAD = 4

SLOT = LANES
N_SPLIT = 3
Q_GATE = HEAD_DIM
K_GATE = HEAD_DIM + N_SPLIT
D_FOX_SLOTS = FOX_HEADS * SLOT
BF16_ROWS = 16
V_ROWS = HEAD_DIM + BF16_ROWS
D_FOX_V = FOX_HEADS * V_ROWS
LOG2_E = math.log2(math.e)
A_POS = HEAD_DIM
A_ONE = HEAD_DIM + 1
A_PAR = HEAD_DIM + 2
D_SWA_SLOTS = SWA_KV_HEADS * SLOT
SWA_PAIR = 2

OFF_QA = 0
OFF_KA = OFF_QA + D_SWA
OFF_VA = OFF_KA + D_SWA_KV
OFF_QB = OFF_VA + D_SWA_KV
OFF_KB = OFF_QB + D_FOX
OFF_VB = OFF_KB + D_FOX
N_MAIN = OFF_VB + D_FOX


def _layer_norm(z, g, b):
    mu = jnp.mean(z, axis=-1, keepdims=True)
    zc = z - mu
    var = jnp.mean(zc * zc, axis=-1, keepdims=True)
    return zc * lax.rsqrt(var + LN_EPS) * g + b


def _nt_dot(a, b):
    return lax.dot_general(a, b, (((1,), (1,)), ((), ())), preferred_element_type=F32)


def _split3(x):
    hi = x.astype(BF16).astype(F32)
    r = x - hi
    mid = r.astype(BF16).astype(F32)
    lo = (r - mid).astype(BF16).astype(F32)
    return hi, mid, lo


def _mods_kernel(c_ref, w_ref, b_ref, o_ref):
    c = c_ref[...]
    sc = c * jax.nn.sigmoid(c)
    o_ref[...] = jnp.dot(sc, w_ref[...], preferred_element_type=F32,
                         precision=lax.Precision.HIGHEST) + b_ref[...]


def _mods(c, w_ada, b_ada):
    bsz, d = c.shape
    n = w_ada.shape[1]
    tn = 1024
    return pl.pallas_call(
        _mods_kernel,
        grid=(n // tn,),
        in_specs=[pl.BlockSpec((bsz, d), lambda j: (0, 0)),
                  pl.BlockSpec((d, tn), lambda j: (0, j)),
                  pl.BlockSpec((1, tn), lambda j: (0, j))],
        out_specs=pl.BlockSpec((bsz, tn), lambda j: (0, j)),
        out_shape=jax.ShapeDtypeStruct((bsz, n), F32),
        compiler_params=pltpu.CompilerParams(dimension_semantics=("arbitrary",),
                                             vmem_limit_bytes=VMEM_LIMIT),
        name="adaln_mods",
    )(c, w_ada, b_ada.reshape(1, n))


def _ffn_kernel(x_ref, mods_ref, wg_ref, wu_ref, wd_ref, lng_ref, lnb_ref, o_ref,
                h_ref, a_ref, *, mod_base):
    x = x_ref[...]
    shift = mods_ref[mod_base:mod_base + 1, :]
    scale = mods_ref[mod_base + 1:mod_base + 2, :]
    gate = mods_ref[mod_base + 2:mod_base + 3, :]
    h_ref[...] = (x * (1.0 + scale) + shift).astype(BF16)
    d_ff = wg_ref.shape[1]
    for c0 in range(0, d_ff, FFN_TF):
        h = h_ref[...]
        g = jnp.dot(h, wg_ref[:, c0:c0 + FFN_TF], preferred_element_type=F32)
        u = jnp.dot(h, wu_ref[:, c0:c0 + FFN_TF], preferred_element_type=F32)
        a_ref[:, c0:c0 + FFN_TF] = (g * jax.nn.sigmoid(g) * u).astype(BF16)
    y = jnp.dot(a_ref[...], wd_ref[...], preferred_element_type=F32)
    z = ALPHA * x + (1.0 + gate) * (MACARON * y)
    o_ref[...] = _layer_norm(z, lng_ref[...], lnb_ref[...])


def _ffn(x, mods, wg, wu, wd, ln_g, ln_b, mod_base):
    bsz, seq, d = x.shape
    d_ff = wg.shape[1]
    const = lambda b, i: (0, 0)
    return pl.pallas_call(
        functools.partial(_ffn_kernel, mod_base=mod_base),
        grid=(bsz, seq // FFN_TM),
        in_specs=[pl.BlockSpec((None, FFN_TM, d), lambda b, i: (b, i, 0)),
                  pl.BlockSpec((None, N_MODS, d), lambda b, i: (b, 0, 0)),
                  pl.BlockSpec((d, d_ff), const, pipeline_mode=pl.Buffered(1)),
                  pl.BlockSpec((d, d_ff), const, pipeline_mode=pl.Buffered(1)),
                  pl.BlockSpec((d_ff, d), const, pipeline_mode=pl.Buffered(1)),
                  pl.BlockSpec((1, d), const),
                  pl.BlockSpec((1, d), const)],
        out_specs=pl.BlockSpec((None, FFN_TM, d), lambda b, i: (b, i, 0)),
        out_shape=jax.ShapeDtypeStruct((bsz, seq, d), F32),
        scratch_shapes=[pltpu.VMEM((FFN_TM, d), BF16),
                        pltpu.VMEM((FFN_TM, d_ff), BF16)],
        compiler_params=pltpu.CompilerParams(dimension_semantics=("arbitrary", "arbitrary"),
                                             vmem_limit_bytes=VMEM_LIMIT),
        name=f"ffn_mod{mod_base}",
    )(x, mods, wg, wu, wd, ln_g.reshape(1, d), ln_b.reshape(1, d))


def _gate_placement():
    place = np.zeros((N_SPLIT * LANES, D_FOX_SLOTS), np.float32)
    ones = np.zeros((1, D_FOX_SLOTS), np.float32)
    for h in range(FOX_HEADS):
        for t in range(N_SPLIT):
            place[t * LANES + h, h * SLOT + K_GATE + t] = 1.0
            ones[0, h * SLOT + Q_GATE + t] = 1.0
    return jnp.asarray(place, BF16), jnp.asarray(ones, F32)


def _inproj_kernel(x_ref, mods_ref, w_ref, wt_ref, wf_ref, bf_ref, place_ref, ones_ref,
                   qat_ref, ka_ref, vat_ref, qbt_ref, kb_ref, vbt_ref,
                   carry_ref):
    @pl.when(pl.program_id(1) == 0)
    def _():
        carry_ref[...] = jnp.zeros_like(carry_ref)

    x = x_ref[...]
    tm = x.shape[0]
    shift = mods_ref[3:4, :]
    scale = mods_ref[4:5, :]
    h = (x * (1.0 + scale) + shift).astype(BF16)

    f = jnp.dot(h, wf_ref[...], preferred_element_type=F32) + bf_ref[...]
    cum = jnp.minimum(f, 0.0) - jnp.log1p(jnp.exp(-jnp.abs(f)))
    row = lax.broadcasted_iota(jnp.int32, cum.shape, 0)
    k = 1
    while k < tm:
        cum = cum + jnp.where(row >= k, pltpu.roll(cum, k, axis=0), 0.0)
        k *= 2
    cum = cum + carry_ref[...]
    carry_ref[...] = cum[tm - 1:tm, :]
    cum = cum * LOG2_E
    cum_t = cum.T[0:SUBLANES, :]

    qat = (_nt_dot(wt_ref[OFF_QA:OFF_QA + D_SWA, :], h) * QK_SCALE).astype(BF16)
    pw = SWA_PAIR * BLOCK
    for hh in range(SWA_Q_HEADS):
        pair, e = hh // SWA_PAIR, hh % SWA_PAIR
        for blk in range(tm // BLOCK):
            qat_ref[pair * HEAD_DIM:(pair + 1) * HEAD_DIM,
                    blk * pw + e * BLOCK:blk * pw + (e + 1) * BLOCK] = (
                qat[hh * HEAD_DIM:(hh + 1) * HEAD_DIM, blk * BLOCK:(blk + 1) * BLOCK])
    ka = jnp.dot(h, w_ref[:, OFF_KA:OFF_KA + D_SWA_KV], preferred_element_type=F32)
    lane = lax.broadcasted_iota(jnp.int32, (tm, LANES), 1)
    krow = lax.broadcasted_iota(jnp.int32, (tm, LANES), 0)
    pos = (krow % BLOCK).astype(F32)
    parity = ((krow // BLOCK) % 2).astype(F32)
    swa_tail = jnp.where(lane == A_POS, pos,
                         jnp.where(lane == A_ONE, 1.0, jnp.where(lane == A_PAR, parity, 0.0)))
    for g, src in ((0, ka), (1, pltpu.roll(ka, HEAD_DIM, axis=1))):
        ka_ref[:, g * SLOT:(g + 1) * SLOT] = jnp.where(lane < HEAD_DIM, src, swa_tail).astype(BF16)
    vat_ref[...] = _nt_dot(wt_ref[OFF_VA:OFF_VA + D_SWA_KV, :], h).astype(BF16)

    qbt = _nt_dot(wt_ref[OFF_QB:OFF_QB + D_FOX, :], h) * (QK_SCALE * LOG2_E)
    q_hi, q_mid, q_lo = _split3(cum_t)
    sub = lax.broadcasted_iota(jnp.int32, (SUBLANES, tm), 0)
    for hh in range(FOX_HEADS):
        extra = jnp.where(sub == 0, q_hi[hh:hh + 1, :],
                jnp.where(sub == 1, q_mid[hh:hh + 1, :],
                jnp.where(sub == 2, q_lo[hh:hh + 1, :],
                jnp.where(sub < 2 * N_SPLIT, 1.0, 0.0))))
        tail = jnp.concatenate(
            [extra, jnp.zeros((SLOT - HEAD_DIM - SUBLANES, tm), F32)], axis=0)
        qbt_ref[hh * SLOT:hh * SLOT + HEAD_DIM, :] = (
            qbt[hh * HEAD_DIM:(hh + 1) * HEAD_DIM, :].astype(BF16))
        qbt_ref[hh * SLOT + HEAD_DIM:(hh + 1) * SLOT, :] = tail.astype(BF16)

    kb = jnp.dot(h, w_ref[:, OFF_KB:OFF_KB + D_FOX], preferred_element_type=F32)
    k_hi, k_mid, k_lo = _split3(-cum)
    parts = jnp.concatenate([k_hi, k_mid, k_lo], axis=-1).astype(BF16)
    gates = jnp.dot(parts, place_ref[...], preferred_element_type=F32) + ones_ref[...]
    low = lax.broadcasted_iota(jnp.int32, (tm, LANES), 1) < HEAD_DIM
    for pair in range(FOX_HEADS // 2):
        kp = kb[:, pair * LANES:(pair + 1) * LANES]
        odd = pltpu.roll(kp, HEAD_DIM, axis=1)
        for e, src in ((0, kp), (1, odd)):
            hh = 2 * pair + e
            kb_ref[:, hh * SLOT:(hh + 1) * SLOT] = jnp.where(
                low, src, gates[:, hh * SLOT:(hh + 1) * SLOT]).astype(BF16)

    vbt = _nt_dot(wt_ref[OFF_VB:OFF_VB + D_FOX, :], h).astype(BF16)
    ones_row = (lax.broadcasted_iota(jnp.int32, (V_ROWS - HEAD_DIM, ATT_TK), 0) == 0).astype(BF16)
    for r in range(tm // ATT_TK):
        for hh in range(FOX_HEADS):
            vbt_ref[r, hh * V_ROWS:hh * V_ROWS + HEAD_DIM, :] = (
                vbt[hh * HEAD_DIM:(hh + 1) * HEAD_DIM, r * ATT_TK:(r + 1) * ATT_TK])
            vbt_ref[r, hh * V_ROWS + HEAD_DIM:(hh + 1) * V_ROWS, :] = ones_row


def _inproj(x1, mods, w_main, w_main_t, w_f, b_f):
    bsz, seq, d = x1.shape
    tm = PROJ_TM
    place, ones = _gate_placement()
    const = lambda b, i: (0, 0)
    whole = lambda a: pl.BlockSpec(a.shape, const, pipeline_mode=pl.Buffered(1))
    rows = lambda width: pl.BlockSpec((None, tm, width), lambda b, i: (b, i, 0))
    cols = lambda height, width=tm: pl.BlockSpec((None, height, width), lambda b, i: (b, 0, i))
    return pl.pallas_call(
        _inproj_kernel,
        grid=(bsz, seq // tm),
        in_specs=[rows(d),
                  pl.BlockSpec((None, N_MODS, d), lambda b, i: (b, 0, 0)),
                  whole(w_main), whole(w_main_t), whole(w_f), whole(b_f),
                  whole(place), whole(ones)],
        out_specs=[cols(D_SWA // SWA_PAIR, SWA_PAIR * tm), rows(D_SWA_SLOTS), cols(D_SWA_KV),
                   cols(D_FOX_SLOTS), rows(D_FOX_SLOTS),
                   pl.BlockSpec((None, tm // ATT_TK, D_FOX_V, ATT_TK), lambda b, i: (b, i, 0, 0))],
        out_shape=[jax.ShapeDtypeStruct((bsz, D_SWA // SWA_PAIR, SWA_PAIR * seq), BF16),
                   jax.ShapeDtypeStruct((bsz, seq, D_SWA_SLOTS), BF16),
                   jax.ShapeDtypeStruct((bsz, D_SWA_KV, seq), BF16),
                   jax.ShapeDtypeStruct((bsz, D_FOX_SLOTS, seq), BF16),
                   jax.ShapeDtypeStruct((bsz, seq, D_FOX_SLOTS), BF16),
                   jax.ShapeDtypeStruct((bsz, seq // ATT_TK, D_FOX_V, ATT_TK), BF16)],
        scratch_shapes=[pltpu.VMEM((1, LANES), F32)],
        compiler_params=pltpu.CompilerParams(dimension_semantics=("arbitrary", "arbitrary"),
                                             vmem_limit_bytes=VMEM_LIMIT),
        name="in_proj",
    )(x1, mods, w_main, w_main_t, w_f, b_f, place, ones)


def _alibi_rows():
    rows = np.zeros((SWA_Q_HEADS // SWA_PAIR, 2, HEAD_DIM, SWA_PAIR * BLOCK), np.float32)
    t = np.arange(BLOCK, dtype=np.float32)
    for hh in range(SWA_Q_HEADS):
        pair, e = hh // SWA_PAIR, hh % SWA_PAIR
        slope = 2.0 ** (-8.0 * (hh + 1) / SWA_Q_HEADS)
        cols = slice(e * BLOCK, (e + 1) * BLOCK)
        for pq in range(2):
            rows[pair, pq, A_POS - HEAD_DIM, cols] = slope
            rows[pair, pq, A_ONE - HEAD_DIM, cols] = -slope * (t + BLOCK * pq)
            rows[pair, pq, A_PAR - HEAD_DIM, cols] = -slope * BLOCK * (1 - 2 * pq)
    return jnp.asarray(rows, BF16)


def _window_bias():
    kpos = np.arange(2 * BLOCK)[:, None] - BLOCK
    qpos = np.arange(SWA_PAIR * BLOCK)[None, :] % BLOCK
    dist = qpos - kpos
    inside = (dist >= 0) & (dist < WINDOW)
    both = np.stack([inside, inside & (kpos >= 0)])
    return jnp.asarray(np.where(both, 0.0, NEG_BIG), F32)


def _attn_kernel(sinks_ref, x1_ref, mods_ref, qat_ref, kac_ref, kap_ref, vatc_ref, vatp_ref,
                 alibi_ref, winbias_ref, winbias0_ref, qbt_ref, kb_ref, vbt_ref, gain_ref, wout_ref,
                 lng_ref, lnb_ref, o_ref, m_sc, acc_sc, s_sc, ot_sc):
    i = pl.program_id(1)
    tq, tk = ATT_TQ, ATT_TK

    pw = SWA_PAIR * BLOCK
    head_of_lane = lax.broadcasted_iota(jnp.int32, (1, pw), 1) // BLOCK
    units = [(nb, g, c) for nb in range(tq // BLOCK) for g in range(SWA_KV_HEADS)
             for c in range(SWA_GROUP // SWA_PAIR)]

    def swa_scores(nb, g, c):
        r0 = nb * BLOCK
        pair = (g * SWA_GROUP) // SWA_PAIR + c
        k_prev = kap_ref[:, g * SLOT:(g + 1) * SLOT] if nb == 0 else (
            kac_ref[r0 - BLOCK:r0, g * SLOT:(g + 1) * SLOT])
        kk = jnp.concatenate([k_prev, kac_ref[r0:r0 + BLOCK, g * SLOT:(g + 1) * SLOT]], axis=0)
        qop = jnp.concatenate([qat_ref[pair * HEAD_DIM:(pair + 1) * HEAD_DIM, nb * pw:(nb + 1) * pw],
                               alibi_ref[pair, nb % 2]], axis=0)
        return jnp.dot(kk, qop, preferred_element_type=F32)

    def fox_scores(j, hh):
        return jnp.dot(kb_ref[j, :, hh * SLOT:(hh + 1) * SLOT],
                       qbt_ref[hh * SLOT:(hh + 1) * SLOT, :],
                       preferred_element_type=F32)

    for u in range(FOX_LOOKAHEAD):
        s_sc[u] = swa_scores(*units[u])
    for u, (nb, g, c) in enumerate(units):
        s = s_sc[u % FOX_LOOKAHEAD]
        ahead = u + FOX_LOOKAHEAD
        if ahead < len(units):
            s_sc[ahead % FOX_LOOKAHEAD] = swa_scores(*units[ahead])
        else:
            s_sc[ahead % FOX_LOOKAHEAD] = fox_scores(0, ahead - len(units))
        r0 = nb * BLOCK
        h0 = g * SWA_GROUP + c * SWA_PAIR
        s = s + (winbias0_ref[...] if nb == 0 else winbias_ref[...])
        sink = jnp.zeros((1, pw), F32)
        for e in range(SWA_PAIR):
            sink = jnp.where(head_of_lane == e, sinks_ref[h0 + e], sink)
        m = jnp.maximum(jnp.max(s, axis=0, keepdims=True), sink)
        p = jnp.exp(s - m)
        denom = jnp.sum(p, axis=0, keepdims=True) + jnp.exp(sink - m)
        v_prev = vatp_ref[g * HEAD_DIM:(g + 1) * HEAD_DIM, :] if nb == 0 else (
            vatc_ref[g * HEAD_DIM:(g + 1) * HEAD_DIM, r0 - BLOCK:r0])
        vv = jnp.concatenate(
            [v_prev, vatc_ref[g * HEAD_DIM:(g + 1) * HEAD_DIM, r0:r0 + BLOCK]], axis=1)
        o = jnp.dot(vv, p.astype(BF16), preferred_element_type=F32) / denom
        for e in range(SWA_PAIR):
            ot_sc[(h0 + e) * HEAD_DIM:(h0 + e + 1) * HEAD_DIM, r0:r0 + BLOCK] = (
                o[:, e * BLOCK:(e + 1) * BLOCK])

    m_sc[...] = jnp.full(m_sc.shape, NEG_BIG, F32)
    acc_sc[...] = jnp.zeros(acc_sc.shape, F32)

    def fox_block(j, causal, has_next):
        pending = {}
        for hh in range(FOX_HEADS):
            s = s_sc[hh] if hh < FOX_LOOKAHEAD else pending.pop(hh)
            ahead = hh + FOX_LOOKAHEAD
            if ahead < FOX_HEADS:
                pending[ahead] = fox_scores(j, ahead)
            elif has_next:
                s_sc[ahead - FOX_HEADS] = fox_scores(j + 1, ahead - FOX_HEADS)
            if causal is not None:
                s = jnp.where(causal, s, NEG_BIG)
            m_prev = m_sc[hh]
            m_new = jnp.maximum(m_prev, jnp.max(s, axis=0, keepdims=True))
            a = jnp.exp2(m_prev - m_new)
            p = jnp.exp2(s - m_new)
            acc_sc[hh] = a * acc_sc[hh] + jnp.dot(
                vbt_ref[j, hh * V_ROWS:(hh + 1) * V_ROWS, :], p.astype(BF16),
                preferred_element_type=F32)
            m_sc[hh] = m_new

    def body(j, carry):
        fox_block(j, None, True)
        return carry

    lax.fori_loop(0, i, body, 0)
    kk_i = lax.broadcasted_iota(jnp.int32, (tk, tq), 0)
    qq_i = lax.broadcasted_iota(jnp.int32, (tk, tq), 1)
    fox_block(i, kk_i <= qq_i, False)

    for hh in range(FOX_HEADS):
        ot_sc[D_SWA + hh * HEAD_DIM:D_SWA + (hh + 1) * HEAD_DIM, :] = (
            acc_sc[hh, 0:HEAD_DIM, :] / acc_sc[hh, HEAD_DIM:HEAD_DIM + 1, :])

    o = ot_sc[...].T
    oa = o[:, 0:D_SWA]
    ob = o[:, D_SWA:D_SWA + D_FOX]
    oa = oa * lax.rsqrt(jnp.mean(oa * oa, axis=-1, keepdims=True) + RMS_EPS) * gain_ref[:, 0:D_SWA]
    ob = ob * lax.rsqrt(jnp.mean(ob * ob, axis=-1, keepdims=True) + RMS_EPS) * gain_ref[:, D_SWA:]
    on = jnp.concatenate([oa, ob], axis=-1).astype(BF16)
    y = jnp.dot(on, wout_ref[...], preferred_element_type=F32)
    gate = mods_ref[5:6, :]
    z = ALPHA * x1_ref[...] + (1.0 + gate) * y
    o_ref[...] = _layer_norm(z, lng_ref[...], lnb_ref[...])


def _attention(x1, mods, qat, ka, vat, qbt, kb, vbt, sinks, gain, w_out, ln_g, ln_b):
    bsz, seq, d = x1.shape
    tq = ATT_TQ
    blocks_per_tile = tq // BLOCK
    prev_blk = lambda i: jnp.maximum(i * blocks_per_tile - 1, 0)
    const = lambda b, i, s: (0, 0)
    kb_blocks = kb.reshape(bsz, seq // ATT_TK, ATT_TK, D_FOX_SLOTS)
    alibi = _alibi_rows()
    winbias = _window_bias()
    grid_spec = pltpu.PrefetchScalarGridSpec(
        num_scalar_prefetch=1,
        grid=(bsz, seq // tq),
        in_specs=[pl.BlockSpec((None, tq, d), lambda b, i, s: (b, i, 0)),
                  pl.BlockSpec((None, N_MODS, d), lambda b, i, s: (b, 0, 0)),
                  pl.BlockSpec((None, D_SWA // SWA_PAIR, SWA_PAIR * tq), lambda b, i, s: (b, 0, i)),
                  pl.BlockSpec((None, tq, D_SWA_SLOTS), lambda b, i, s: (b, i, 0)),
                  pl.BlockSpec((None, BLOCK, D_SWA_SLOTS), lambda b, i, s: (b, prev_blk(i), 0)),
                  pl.BlockSpec((None, D_SWA_KV, tq), lambda b, i, s: (b, 0, i)),
                  pl.BlockSpec((None, D_SWA_KV, BLOCK), lambda b, i, s: (b, 0, prev_blk(i))),
                  pl.BlockSpec(alibi.shape, lambda b, i, s: (0, 0, 0, 0)),
                  pl.BlockSpec((None,) + winbias.shape[1:], lambda b, i, s: (0, 0, 0)),
                  pl.BlockSpec((None,) + winbias.shape[1:],
                               lambda b, i, s: (jnp.where(i == 0, 1, 0), 0, 0)),
                  pl.BlockSpec((None, D_FOX_SLOTS, tq), lambda b, i, s: (b, 0, i)),
                  pl.BlockSpec((None, seq // ATT_TK, ATT_TK, D_FOX_SLOTS),
                               lambda b, i, s: (b, 0, 0, 0)),
                  pl.BlockSpec((None, seq // ATT_TK, D_FOX_V, ATT_TK),
                               lambda b, i, s: (b, 0, 0, 0)),
                  pl.BlockSpec((1, d), const),
                  pl.BlockSpec((d, d), const),
                  pl.BlockSpec((1, d), const),
                  pl.BlockSpec((1, d), const)],
        out_specs=pl.BlockSpec((None, tq, d), lambda b, i, s: (b, i, 0)),
        scratch_shapes=[pltpu.VMEM((FOX_HEADS, 1, tq), F32),
                        pltpu.VMEM((FOX_HEADS, V_ROWS, tq), F32),
                        pltpu.VMEM((FOX_LOOKAHEAD, ATT_TK, tq), F32),
                        pltpu.VMEM((d, tq), F32)],
    )
    return pl.pallas_call(
        _attn_kernel,
        grid_spec=grid_spec,
        out_shape=jax.ShapeDtypeStruct((bsz, seq, d), F32),
        compiler_params=pltpu.CompilerParams(dimension_semantics=("arbitrary", "arbitrary"),
                                             vmem_limit_bytes=VMEM_LIMIT),
        name="token_mix",
    )(sinks, x1, mods, qat, ka, ka, vat, vat, alibi, winbias, winbias, qbt, kb_blocks, vbt,
      gain.reshape(1, d), w_out, ln_g.reshape(1, d), ln_b.reshape(1, d))


def kernel(x, c, w_ada, b_ada, ffn1_w_gate, ffn1_w_up, ffn1_w_down, w_in, b_forget,
           swa_sinks, grp_gain, w_out, ffn2_w_gate, ffn2_w_up, ffn2_w_down,
           ln1_g, ln1_b, ln2_g, ln2_b, ln3_g, ln3_b):
    bsz, seq, d = x.shape
    assert w_ada.shape[0] == DEPTH == 1
    assert seq % FFN_TM == 0 and seq % PROJ_TM == 0 and seq % ATT_TQ == 0
    assert PROJ_TM % ATT_TK == 0 and ATT_TQ == ATT_TK and ATT_TQ % BLOCK == 0
    assert w_in.shape[2] == N_MAIN + FOX_HEADS and D_SWA + D_FOX == d
    assert SWA_Q_HEADS == 8 and WINDOW == BLOCK
    assert PROJ_TM % (2 * BLOCK) == 0 and ATT_TQ % (2 * BLOCK) == 0
    l = 0
    mods = _mods(c, w_ada[l], b_ada[l]).reshape(bsz, N_MODS, d)

    x1 = _ffn(x, mods, ffn1_w_gate[l].astype(BF16), ffn1_w_up[l].astype(BF16),
              ffn1_w_down[l].astype(BF16), ln1_g[l], ln1_b[l], mod_base=0)

    w_main = w_in[l][:, :N_MAIN].astype(BF16)
    w_f = jnp.pad(w_in[l][:, N_MAIN:], ((0, 0), (0, LANES - FOX_HEADS))).astype(BF16)
    b_f = jnp.pad(b_forget[l], (0, LANES - FOX_HEADS)).reshape(1, LANES)
    qat, ka, vat, qbt, kb, vbt = _inproj(x1, mods, w_main, w_main.T, w_f, b_f)

    x2 = _attention(x1, mods, qat, ka, vat, qbt, kb, vbt, swa_sinks[l],
                    grp_gain[l], w_out[l].astype(BF16), ln2_g[l], ln2_b[l])

    return _ffn(x2, mods, ffn2_w_gate[l].astype(BF16), ffn2_w_up[l].astype(BF16),
                ffn2_w_down[l].astype(BF16), ln3_g[l], ln3_b[l], mod_base=6)
```

```python
import functools
import math

import numpy as np
import jax
import jax.numpy as jnp
from jax import lax
from jax.experimental import pallas as pl
from jax.experimental.pallas import tpu as pltpu

F32 = jnp.float32
BF16 = jnp.bfloat16

SWA_Q_HEADS = 8
SWA_KV_HEADS = 2
SWA_GROUP = SWA_Q_HEADS // SWA_KV_HEADS
FOX_HEADS = 8
HEAD_DIM = 64
D_SWA = SWA_Q_HEADS * HEAD_DIM
D_SWA_KV = SWA_KV_HEADS * HEAD_DIM
D_FOX = FOX_HEADS * HEAD_DIM
WINDOW = 128
BLOCK = 128
MACARON = 0.5
DEPTH = 1
ALPHA = (2.0 * DEPTH) ** 0.25
LN_EPS = 1e-5
RMS_EPS = 1e-6
N_MODS = 9
QK_SCALE = 1.0 / math.sqrt(HEAD_DIM)

LANES = 128
SUBLANES = 8
VMEM_LIMIT = 56 * 1024 * 1024

NEG_BIG = -1e30

FFN_TM = 512
FFN_TF = 256
PROJ_TM = 512
ATT_TQ = 256
ATT_TK = 256
FOX_LOOKAHEAD = 4

SLOT = LANES
N_SPLIT = 3
Q_GATE = HEAD_DIM
K_GATE = HEAD_DIM + N_SPLIT
K_STRIDE = FOX_HEADS
D_FOX_SLOTS = FOX_HEADS * SLOT
BF16_ROWS = 16
V_ROWS = HEAD_DIM + BF16_ROWS
D_FOX_V = FOX_HEADS * V_ROWS
LOG2_E = math.log2(math.e)
A_POS = HEAD_DIM
A_ONE = HEAD_DIM + 1
A_PAR = HEAD_DIM + 2
D_SWA_SLOTS = SWA_KV_HEADS * SLOT
SWA_PAIR = 2

OFF_QA = 0
OFF_KA = OFF_QA + D_SWA
OFF_VA = OFF_KA + D_SWA_KV
OFF_QB = OFF_VA + D_SWA_KV
OFF_KB = OFF_QB + D_FOX
OFF_VB = OFF_KB + D_FOX
N_MAIN = OFF_VB + D_FOX


def _layer_norm(z, g, b):
    mu = jnp.mean(z, axis=-1, keepdims=True)
    zc = z - mu
    var = jnp.mean(zc * zc, axis=-1, keepdims=True)
    return zc * lax.rsqrt(var + LN_EPS) * g + b


def _nt_dot(a, b):
    return lax.dot_general(a, b, (((1,), (1,)), ((), ())), preferred_element_type=F32)


def _split3(x):
    hi = x.astype(BF16).astype(F32)
    r = x - hi
    mid = r.astype(BF16).astype(F32)
    lo = (r - mid).astype(BF16).astype(F32)
    return hi, mid, lo


def _mods_kernel(c_ref, w_ref, b_ref, o_ref):
    c = c_ref[...]
    sc = c * jax.nn.sigmoid(c)
    o_ref[...] = jnp.dot(sc, w_ref[...], preferred_element_type=F32,
                         precision=lax.Precision.HIGHEST) + b_ref[...]


def _mods(c, w_ada, b_ada):
    bsz, d = c.shape
    n = w_ada.shape[1]
    tn = 1024
    return pl.pallas_call(
        _mods_kernel,
        grid=(n // tn,),
        in_specs=[pl.BlockSpec((bsz, d), lambda j: (0, 0)),
                  pl.BlockSpec((d, tn), lambda j: (0, j)),
                  pl.BlockSpec((1, tn), lambda j: (0, j))],
        out_specs=pl.BlockSpec((bsz, tn), lambda j: (0, j)),
        out_shape=jax.ShapeDtypeStruct((bsz, n), F32),
        compiler_params=pltpu.CompilerParams(dimension_semantics=("arbitrary",),
                                             vmem_limit_bytes=VMEM_LIMIT),
        name="adaln_mods",
    )(c, w_ada, b_ada.reshape(1, n))


def _ffn_kernel(x_ref, mods_ref, wg_ref, wu_ref, wd_ref, lng_ref, lnb_ref, o_ref,
                h_ref, a_ref, *, mod_base):
    x = x_ref[...]
    shift = mods_ref[mod_base:mod_base + 1, :]
    scale = mods_ref[mod_base + 1:mod_base + 2, :]
    gate = mods_ref[mod_base + 2:mod_base + 3, :]
    h_ref[...] = (x * (1.0 + scale) + shift).astype(BF16)
    d_ff = wg_ref.shape[1]
    for c0 in range(0, d_ff, FFN_TF):
        h = h_ref[...]
        g = jnp.dot(h, wg_ref[:, c0:c0 + FFN_TF], preferred_element_type=F32)
        u = jnp.dot(h, wu_ref[:, c0:c0 + FFN_TF], preferred_element_type=F32)
        a_ref[:, c0:c0 + FFN_TF] = (g * jax.nn.sigmoid(g) * u).astype(BF16)
    half = x.shape[0] // 2
    gate_half = (1.0 + gate) * MACARON
    for r0 in (0, half):
        y = jnp.dot(a_ref[r0:r0 + half, :], wd_ref[...], preferred_element_type=F32)
        z = ALPHA * x[r0:r0 + half, :] + gate_half * y
        o_ref[r0:r0 + half, :] = _layer_norm(z, lng_ref[...], lnb_ref[...])


def _ffn(x, mods, wg, wu, wd, ln_g, ln_b, mod_base):
    bsz, seq, d = x.shape
    d_ff = wg.shape[1]
    const = lambda b, i: (0, 0)
    return pl.pallas_call(
        functools.partial(_ffn_kernel, mod_base=mod_base),
        grid=(bsz, seq // FFN_TM),
        in_specs=[pl.BlockSpec((None, FFN_TM, d), lambda b, i: (b, i, 0)),
                  pl.BlockSpec((None, N_MODS, d), lambda b, i: (b, 0, 0)),
                  pl.BlockSpec((d, d_ff), const, pipeline_mode=pl.Buffered(1)),
                  pl.BlockSpec((d, d_ff), const, pipeline_mode=pl.Buffered(1)),
                  pl.BlockSpec((d_ff, d), const, pipeline_mode=pl.Buffered(1)),
                  pl.BlockSpec((1, d), const),
                  pl.BlockSpec((1, d), const)],
        out_specs=pl.BlockSpec((None, FFN_TM, d), lambda b, i: (b, i, 0)),
        out_shape=jax.ShapeDtypeStruct((bsz, seq, d), F32),
        scratch_shapes=[pltpu.VMEM((FFN_TM, d), BF16),
                        pltpu.VMEM((FFN_TM, d_ff), BF16)],
        compiler_params=pltpu.CompilerParams(dimension_semantics=("arbitrary", "arbitrary"),
                                             vmem_limit_bytes=VMEM_LIMIT),
        name=f"ffn_mod{mod_base}",
    )(x, mods, wg, wu, wd, ln_g.reshape(1, d), ln_b.reshape(1, d))


def _inproj_kernel(x_ref, mods_ref, w_ref, wt_ref, wf_ref, bf_ref,
                   qat_ref, ka_ref, vat_ref, qbt_ref, kb_ref, vbt_ref,
                   carry_ref):
    @pl.when(pl.program_id(1) == 0)
    def _():
        carry_ref[...] = jnp.zeros_like(carry_ref)

    x = x_ref[...]
    tm = x.shape[0]
    shift = mods_ref[3:4, :]
    scale = mods_ref[4:5, :]
    h = (x * (1.0 + scale) + shift).astype(BF16)

    kaf = jnp.dot(h, wf_ref[...], preferred_element_type=F32)
    ka = kaf[:, 0:D_SWA_KV]
    f = kaf[:, D_SWA_KV:D_SWA_KV + LANES] + bf_ref[...]
    cum = jnp.minimum(f, 0.0) - jnp.log1p(jnp.exp(-jnp.abs(f)))
    row = lax.broadcasted_iota(jnp.int32, cum.shape, 0)
    k = 1
    while k < tm:
        cum = cum + jnp.where(row >= k, pltpu.roll(cum, k, axis=0), 0.0)
        k *= 2
    cum = cum + carry_ref[...]
    carry_ref[...] = cum[tm - 1:tm, :]
    cum = cum * LOG2_E
    cum_t = cum.T[0:SUBLANES, :]

    qat = (_nt_dot(wt_ref[OFF_QA:OFF_QA + D_SWA, :], h) * QK_SCALE).astype(BF16)
    pw = SWA_PAIR * BLOCK
    for hh in range(SWA_Q_HEADS):
        pair, e = hh // SWA_PAIR, hh % SWA_PAIR
        for blk in range(tm // BLOCK):
            qat_ref[pair * HEAD_DIM:(pair + 1) * HEAD_DIM,
                    blk * pw + e * BLOCK:blk * pw + (e + 1) * BLOCK] = (
                qat[hh * HEAD_DIM:(hh + 1) * HEAD_DIM, blk * BLOCK:(blk + 1) * BLOCK])
    lane = lax.broadcasted_iota(jnp.int32, (tm, LANES), 1)
    krow = lax.broadcasted_iota(jnp.int32, (tm, LANES), 0)
    pos = (krow % BLOCK).astype(F32)
    parity = ((krow // BLOCK) % 2).astype(F32)
    swa_tail = jnp.where(lane == A_POS, pos,
                         jnp.where(lane == A_ONE, 1.0, jnp.where(lane == A_PAR, parity, 0.0)))
    for g, src in ((0, ka), (1, pltpu.roll(ka, HEAD_DIM, axis=1))):
        ka_ref[:, g * SLOT:(g + 1) * SLOT] = jnp.where(lane < HEAD_DIM, src, swa_tail).astype(BF16)
    vat_ref[...] = _nt_dot(wt_ref[OFF_VA:OFF_VA + D_SWA_KV, :], h).astype(BF16)

    qbt = _nt_dot(wt_ref[OFF_QB:OFF_QB + D_FOX, :], h) * (QK_SCALE * LOG2_E)
    q_hi, q_mid, q_lo = _split3(cum_t)
    sub = lax.broadcasted_iota(jnp.int32, (SUBLANES, tm), 0)
    first_one = K_GATE - HEAD_DIM
    one_row = jnp.where(sub == first_one, 1.0, 0.0)
    for hh in range(FOX_HEADS):
        extra = jnp.where(sub == 0, q_hi[hh:hh + 1, :],
                jnp.where(sub == 1, q_mid[hh:hh + 1, :],
                jnp.where(sub == 2, q_lo[hh:hh + 1, :], one_row)))
        tail = jnp.concatenate(
            [extra] + [one_row] * (N_SPLIT - 1)
            + [jnp.zeros((SLOT - HEAD_DIM - N_SPLIT * SUBLANES, tm), F32)], axis=0)
        qbt_ref[hh * SLOT:hh * SLOT + HEAD_DIM, :] = (
            qbt[hh * HEAD_DIM:(hh + 1) * HEAD_DIM, :].astype(BF16))
        qbt_ref[hh * SLOT + HEAD_DIM:(hh + 1) * SLOT, :] = tail.astype(BF16)

    kb = jnp.dot(h, w_ref[:, OFF_KB:OFF_KB + D_FOX], preferred_element_type=F32)
    k_hi, k_mid, k_lo = _split3(-cum)
    packed = jnp.where(lane < K_STRIDE, k_hi,
             jnp.where(lane < 2 * K_STRIDE, pltpu.roll(k_mid, K_STRIDE, axis=1),
             jnp.where(lane < 3 * K_STRIDE, pltpu.roll(k_lo, 2 * K_STRIDE, axis=1), 0.0)))
    gate_lane = ((lane >= K_GATE) & (lane < K_GATE + N_SPLIT * K_STRIDE)
                 & ((lane - K_GATE) % K_STRIDE == 0))
    ones_lane = jnp.where((lane >= Q_GATE) & (lane < Q_GATE + N_SPLIT), 1.0, 0.0)
    for pair in range(FOX_HEADS // 2):
        kp = kb[:, pair * LANES:(pair + 1) * LANES]
        odd = pltpu.roll(kp, HEAD_DIM, axis=1)
        for e, src in ((0, kp), (1, odd)):
            hh = 2 * pair + e
            gates = pltpu.roll(packed, K_GATE - hh, axis=1)
            tail = jnp.where(gate_lane, gates, ones_lane)
            kb_ref[:, hh * SLOT:(hh + 1) * SLOT] = jnp.where(lane < HEAD_DIM, src, tail).astype(BF16)

    vbt = _nt_dot(wt_ref[OFF_VB:OFF_VB + D_FOX, :], h).astype(BF16)
    ones_row = (lax.broadcasted_iota(jnp.int32, (V_ROWS - HEAD_DIM, ATT_TK), 0) == 0).astype(BF16)
    for r in range(tm // ATT_TK):
        for hh in range(FOX_HEADS):
            vbt_ref[r, hh * V_ROWS:hh * V_ROWS + HEAD_DIM, :] = (
                vbt[hh * HEAD_DIM:(hh + 1) * HEAD_DIM, r * ATT_TK:(r + 1) * ATT_TK])
            vbt_ref[r, hh * V_ROWS + HEAD_DIM:(hh + 1) * V_ROWS, :] = ones_row


def _inproj(x1, mods, w_main, w_main_t, w_f, b_f):
    bsz, seq, d = x1.shape
    tm = PROJ_TM
    const = lambda b, i: (0, 0)
    whole = lambda a: pl.BlockSpec(a.shape, const, pipeline_mode=pl.Buffered(1))
    rows = lambda width: pl.BlockSpec((None, tm, width), lambda b, i: (b, i, 0))
    cols = lambda height, width=tm: pl.BlockSpec((None, height, width), lambda b, i: (b, 0, i))
    return pl.pallas_call(
        _inproj_kernel,
        grid=(bsz, seq // tm),
        in_specs=[rows(d),
                  pl.BlockSpec((None, N_MODS, d), lambda b, i: (b, 0, 0)),
                  whole(w_main), whole(w_main_t), whole(w_f), whole(b_f)],
        out_specs=[cols(D_SWA // SWA_PAIR, SWA_PAIR * tm), rows(D_SWA_SLOTS), cols(D_SWA_KV),
                   cols(D_FOX_SLOTS), rows(D_FOX_SLOTS),
                   pl.BlockSpec((None, tm // ATT_TK, D_FOX_V, ATT_TK), lambda b, i: (b, i, 0, 0))],
        out_shape=[jax.ShapeDtypeStruct((bsz, D_SWA // SWA_PAIR, SWA_PAIR * seq), BF16),
                   jax.ShapeDtypeStruct((bsz, seq, D_SWA_SLOTS), BF16),
                   jax.ShapeDtypeStruct((bsz, D_SWA_KV, seq), BF16),
                   jax.ShapeDtypeStruct((bsz, D_FOX_SLOTS, seq), BF16),
                   jax.ShapeDtypeStruct((bsz, seq, D_FOX_SLOTS), BF16),
                   jax.ShapeDtypeStruct((bsz, seq // ATT_TK, D_FOX_V, ATT_TK), BF16)],
        scratch_shapes=[pltpu.VMEM((1, LANES), F32)],
        compiler_params=pltpu.CompilerParams(dimension_semantics=("arbitrary", "arbitrary"),
                                             vmem_limit_bytes=VMEM_LIMIT),
        name="in_proj",
    )(x1, mods, w_main, w_main_t, w_f, b_f)


def _alibi_rows():
    rows = np.zeros((SWA_Q_HEADS // SWA_PAIR, 2, HEAD_DIM, SWA_PAIR * BLOCK), np.float32)
    t = np.arange(BLOCK, dtype=np.float32)
    for hh in range(SWA_Q_HEADS):
        pair, e = hh // SWA_PAIR, hh % SWA_PAIR
        slope = 2.0 ** (-8.0 * (hh + 1) / SWA_Q_HEADS)
        cols = slice(e * BLOCK, (e + 1) * BLOCK)
        for pq in range(2):
            rows[pair, pq, A_POS - HEAD_DIM, cols] = slope
            rows[pair, pq, A_ONE - HEAD_DIM, cols] = -slope * (t + BLOCK * pq)
            rows[pair, pq, A_PAR - HEAD_DIM, cols] = -slope * BLOCK * (1 - 2 * pq)
    return jnp.asarray(rows, BF16)


def _window_bias():
    kpos = np.arange(2 * BLOCK)[:, None] - BLOCK
    qpos = np.arange(SWA_PAIR * BLOCK)[None, :] % BLOCK
    dist = qpos - kpos
    inside = (dist >= 0) & (dist < WINDOW)
    both = np.stack([inside, inside & (kpos >= 0)])
    return jnp.asarray(np.where(both, 0.0, NEG_BIG), F32)


def _attn_kernel(sinks_ref, x1_ref, mods_ref, qat_ref, kac_ref, kap_ref, vatc_ref, vatp_ref,
                 alibi_ref, winbias_ref, winbias0_ref, qbt_ref, kb_ref, vbt_ref, gain_ref, wout_ref,
                 lng_ref, lnb_ref, o_ref, m_sc, acc_sc, s_sc, ot_sc):
    i = pl.program_id(1)
    tq, tk = ATT_TQ, ATT_TK

    pw = SWA_PAIR * BLOCK
    head_of_lane = lax.broadcasted_iota(jnp.int32, (1, pw), 1) // BLOCK
    units = [(nb, g, c) for nb in range(tq // BLOCK) for g in range(SWA_KV_HEADS)
             for c in range(SWA_GROUP // SWA_PAIR)]

    def swa_scores(nb, g, c):
        r0 = nb * BLOCK
        pair = (g * SWA_GROUP) // SWA_PAIR + c
        k_prev = kap_ref[:, g * SLOT:(g + 1) * SLOT] if nb == 0 else (
            kac_ref[r0 - BLOCK:r0, g * SLOT:(g + 1) * SLOT])
        kk = jnp.concatenate([k_prev, kac_ref[r0:r0 + BLOCK, g * SLOT:(g + 1) * SLOT]], axis=0)
        qop = jnp.concatenate([qat_ref[pair * HEAD_DIM:(pair + 1) * HEAD_DIM, nb * pw:(nb + 1) * pw],
                               alibi_ref[pair, nb % 2]], axis=0)
        return jnp.dot(kk, qop, preferred_element_type=F32)

    def fox_scores(j, hh):
        return jnp.dot(kb_ref[j, :, hh * SLOT:(hh + 1) * SLOT],
                       qbt_ref[hh * SLOT:(hh + 1) * SLOT, :],
                       preferred_element_type=F32)

    for u in range(FOX_LOOKAHEAD):
        s_sc[u] = swa_scores(*units[u])
    for u, (nb, g, c) in enumerate(units):
        s = s_sc[u % FOX_LOOKAHEAD]
        ahead = u + FOX_LOOKAHEAD
        if ahead < len(units):
            s_sc[ahead % FOX_LOOKAHEAD] = swa_scores(*units[ahead])
        else:
            s_sc[ahead % FOX_LOOKAHEAD] = fox_scores(0, ahead - len(units))
        r0 = nb * BLOCK
        h0 = g * SWA_GROUP + c * SWA_PAIR
        s = s + (winbias0_ref[...] if nb == 0 else winbias_ref[...])
        sink = jnp.zeros((1, pw), F32)
        for e in range(SWA_PAIR):
            sink = jnp.where(head_of_lane == e, sinks_ref[h0 + e], sink)
        m = jnp.maximum(jnp.max(s, axis=0, keepdims=True), sink)
        p = jnp.exp(s - m)
        denom = jnp.sum(p, axis=0, keepdims=True) + jnp.exp(sink - m)
        v_prev = vatp_ref[g * HEAD_DIM:(g + 1) * HEAD_DIM, :] if nb == 0 else (
            vatc_ref[g * HEAD_DIM:(g + 1) * HEAD_DIM, r0 - BLOCK:r0])
        vv = jnp.concatenate(
            [v_prev, vatc_ref[g * HEAD_DIM:(g + 1) * HEAD_DIM, r0:r0 + BLOCK]], axis=1)
        o = jnp.dot(vv, p.astype(BF16), preferred_element_type=F32) / denom
        for e in range(SWA_PAIR):
            ot_sc[(h0 + e) * HEAD_DIM:(h0 + e + 1) * HEAD_DIM, r0:r0 + BLOCK] = (
                o[:, e * BLOCK:(e + 1) * BLOCK])

    m_sc[...] = jnp.full(m_sc.shape, NEG_BIG, F32)
    acc_sc[...] = jnp.zeros(acc_sc.shape, F32)

    def fox_block(j, causal, has_next):
        pending = {}
        for hh in range(FOX_HEADS):
            s = s_sc[hh] if hh < FOX_LOOKAHEAD else pending.pop(hh)
            ahead = hh + FOX_LOOKAHEAD
            if ahead < FOX_HEADS:
                pending[ahead] = fox_scores(j, ahead)
            elif has_next:
                s_sc[ahead - FOX_HEADS] = fox_scores(j + 1, ahead - FOX_HEADS)
            if causal is not None:
                s = jnp.where(causal, s, NEG_BIG)
            m_prev = m_sc[hh]
            m_new = jnp.maximum(m_prev, jnp.max(s, axis=0, keepdims=True))
            a = jnp.exp2(m_prev - m_new)
            p = jnp.exp2(s - m_new)
            acc_sc[hh] = a * acc_sc[hh] + jnp.dot(
                vbt_ref[j, hh * V_ROWS:(hh + 1) * V_ROWS, :], p.astype(BF16),
                preferred_element_type=F32)
            m_sc[hh] = m_new

    def body(j, carry):
        fox_block(j, None, True)
        return carry

    def body_pair(jj, carry):
        fox_block(2 * jj, None, True)
        fox_block(2 * jj + 1, None, True)
        return carry

    lax.fori_loop(0, i // 2, body_pair, 0)
    lax.fori_loop(2 * (i // 2), i, body, 0)
    kk_i = lax.broadcasted_iota(jnp.int32, (tk, tq), 0)
    qq_i = lax.broadcasted_iota(jnp.int32, (tk, tq), 1)
    fox_block(i, kk_i <= qq_i, False)

    for hh in range(FOX_HEADS):
        ot_sc[D_SWA + hh * HEAD_DIM:D_SWA + (hh + 1) * HEAD_DIM, :] = (
            acc_sc[hh, 0:HEAD_DIM, :] / acc_sc[hh, HEAD_DIM:HEAD_DIM + 1, :])

    o = ot_sc[...].T
    oa = o[:, 0:D_SWA]
    ob = o[:, D_SWA:D_SWA + D_FOX]
    oa = oa * lax.rsqrt(jnp.mean(oa * oa, axis=-1, keepdims=True) + RMS_EPS) * gain_ref[:, 0:D_SWA]
    ob = ob * lax.rsqrt(jnp.mean(ob * ob, axis=-1, keepdims=True) + RMS_EPS) * gain_ref[:, D_SWA:]
    on = jnp.concatenate([oa, ob], axis=-1).astype(BF16)
    y = jnp.dot(on, wout_ref[...], preferred_element_type=F32)
    gate = mods_ref[5:6, :]
    z = ALPHA * x1_ref[...] + (1.0 + gate) * y
    o_ref[...] = _layer_norm(z, lng_ref[...], lnb_ref[...])


def _attention(x1, mods, qat, ka, vat, qbt, kb, vbt, sinks, gain, w_out, ln_g, ln_b):
    bsz, seq, d = x1.shape
    tq = ATT_TQ
    blocks_per_tile = tq // BLOCK
    prev_blk = lambda i: jnp.maximum(i * blocks_per_tile - 1, 0)
    const = lambda b, i, s: (0, 0)
    kb_blocks = kb.reshape(bsz, seq // ATT_TK, ATT_TK, D_FOX_SLOTS)
    alibi = _alibi_rows()
    winbias = _window_bias()
    grid_spec = pltpu.PrefetchScalarGridSpec(
        num_scalar_prefetch=1,
        grid=(bsz, seq // tq),
        in_specs=[pl.BlockSpec((None, tq, d), lambda b, i, s: (b, i, 0)),
                  pl.BlockSpec((None, N_MODS, d), lambda b, i, s: (b, 0, 0)),
                  pl.BlockSpec((None, D_SWA // SWA_PAIR, SWA_PAIR * tq), lambda b, i, s: (b, 0, i)),
                  pl.BlockSpec((None, tq, D_SWA_SLOTS), lambda b, i, s: (b, i, 0)),
                  pl.BlockSpec((None, BLOCK, D_SWA_SLOTS), lambda b, i, s: (b, prev_blk(i), 0)),
                  pl.BlockSpec((None, D_SWA_KV, tq), lambda b, i, s: (b, 0, i)),
                  pl.BlockSpec((None, D_SWA_KV, BLOCK), lambda b, i, s: (b, 0, prev_blk(i))),
                  pl.BlockSpec(alibi.shape, lambda b, i, s: (0, 0, 0, 0)),
                  pl.BlockSpec((None,) + winbias.shape[1:], lambda b, i, s: (0, 0, 0)),
                  pl.BlockSpec((None,) + winbias.shape[1:],
                               lambda b, i, s: (jnp.where(i == 0, 1, 0), 0, 0)),
                  pl.BlockSpec((None, D_FOX_SLOTS, tq), lambda b, i, s: (b, 0, i)),
                  pl.BlockSpec((None, seq // ATT_TK, ATT_TK, D_FOX_SLOTS),
                               lambda b, i, s: (b, 0, 0, 0)),
                  pl.BlockSpec((None, seq // ATT_TK, D_FOX_V, ATT_TK),
                               lambda b, i, s: (b, 0, 0, 0)),
                  pl.BlockSpec((1, d), const),
                  pl.BlockSpec((d, d), const),
                  pl.BlockSpec((1, d), const),
                  pl.BlockSpec((1, d), const)],
        out_specs=pl.BlockSpec((None, tq, d), lambda b, i, s: (b, i, 0)),
        scratch_shapes=[pltpu.VMEM((FOX_HEADS, 1, tq), F32),
                        pltpu.VMEM((FOX_HEADS, V_ROWS, tq), F32),
                        pltpu.VMEM((FOX_LOOKAHEAD, ATT_TK, tq), F32),
                        pltpu.VMEM((d, tq), F32)],
    )
    return pl.pallas_call(
        _attn_kernel,
        grid_spec=grid_spec,
        out_shape=jax.ShapeDtypeStruct((bsz, seq, d), F32),
        compiler_params=pltpu.CompilerParams(dimension_semantics=("arbitrary", "arbitrary"),
                                             vmem_limit_bytes=VMEM_LIMIT),
        name="token_mix",
    )(sinks, x1, mods, qat, ka, ka, vat, vat, alibi, winbias, winbias, qbt, kb_blocks, vbt,
      gain.reshape(1, d), w_out, ln_g.reshape(1, d), ln_b.reshape(1, d))


def kernel(x, c, w_ada, b_ada, ffn1_w_gate, ffn1_w_up, ffn1_w_down, w_in, b_forget,
           swa_sinks, grp_gain, w_out, ffn2_w_gate, ffn2_w_up, ffn2_w_down,
           ln1_g, ln1_b, ln2_g, ln2_b, ln3_g, ln3_b):
    bsz, seq, d = x.shape
    assert w_ada.shape[0] == DEPTH == 1
    assert seq % FFN_TM == 0 and seq % PROJ_TM == 0 and seq % ATT_TQ == 0
    assert PROJ_TM % ATT_TK == 0 and ATT_TQ == ATT_TK and ATT_TQ % BLOCK == 0
    assert w_in.shape[2] == N_MAIN + FOX_HEADS and D_SWA + D_FOX == d
    assert K_STRIDE == SUBLANES and K_GATE + N_SPLIT * K_STRIDE <= SLOT
    assert SWA_Q_HEADS == 8 and WINDOW == BLOCK
    assert PROJ_TM % (2 * BLOCK) == 0 and ATT_TQ % (2 * BLOCK) == 0
    l = 0
    mods = _mods(c, w_ada[l], b_ada[l]).reshape(bsz, N_MODS, d)

    x1 = _ffn(x, mods, ffn1_w_gate[l].astype(BF16), ffn1_w_up[l].astype(BF16),
              ffn1_w_down[l].astype(BF16), ln1_g[l], ln1_b[l], mod_base=0)

    w_main = w_in[l][:, :N_MAIN].astype(BF16)
    w_f = jnp.concatenate(
        [w_in[l][:, OFF_KA:OFF_KA + D_SWA_KV],
         jnp.pad(w_in[l][:, N_MAIN:], ((0, 0), (0, LANES - FOX_HEADS)))], axis=1).astype(BF16)
    b_f = jnp.pad(b_forget[l], (0, LANES - FOX_HEADS)).reshape(1, LANES)
    qat, ka, vat, qbt, kb, vbt = _inproj(x1, mods, w_main, w_main.T, w_f, b_f)

    x2 = _attention(x1, mods, qat, ka, vat, qbt, kb, vbt, swa_sinks[l],
                    grp_gain[l], w_out[l].astype(BF16), ln2_g[l], ln2_b[l])

    return _ffn(x2, mods, ffn2_w_gate[l].astype(BF16), ffn2_w_up[l].astype(BF16),
                ffn2_w_down[l].astype(BF16), ln3_g[l], ln3_b[l], mod_base=6)
```

```python
import functools
import math

import numpy as np
import jax
import jax.numpy as jnp
from jax import lax
from jax.experimental import pallas as pl
from jax.experimental.pallas import tpu as pltpu

F32 = jnp.float32
BF16 = jnp.bfloat16

SWA_Q_HEADS = 8
SWA_KV_HEADS = 2
SWA_GROUP = SWA_Q_HEADS // SWA_KV_HEADS
FOX_HEADS = 8
HEAD_DIM = 64
D_SWA = SWA_Q_HEADS * HEAD_DIM
D_SWA_KV = SWA_KV_HEADS * HEAD_DIM
D_FOX = FOX_HEADS * HEAD_DIM
WINDOW = 128
BLOCK = 128
MACARON = 0.5
DEPTH = 1
ALPHA = (2.0 * DEPTH) ** 0.25
LN_EPS = 1e-5
RMS_EPS = 1e-6
N_MODS = 9
QK_SCALE = 1.0 / math.sqrt(HEAD_DIM)

LANES = 128
SUBLANES = 8
VMEM_LIMIT = 56 * 1024 * 1024

NEG_BIG = -1e30

FFN_TM = 1024
FFN_TF = 256
FFN_PASSES = 4
PROJ_TM = 512
ATT_TQ = 256
ATT_TK = 256
FOX_LOOKAHEAD = 8

SLOT = LANES
N_SPLIT = 3
Q_GATE = HEAD_DIM
K_GATE = HEAD_DIM + N_SPLIT
K_STRIDE = FOX_HEADS
D_FOX_SLOTS = FOX_HEADS * SLOT
BF16_ROWS = 16
V_ROWS = HEAD_DIM + BF16_ROWS
D_FOX_V = FOX_HEADS * V_ROWS
LOG2_E = math.log2(math.e)
A_POS = HEAD_DIM
A_ONE = A_POS + N_SPLIT
A_PAR = A_ONE + N_SPLIT
D_SWA_SLOTS = SWA_KV_HEADS * SLOT
SWA_PAIR = 2

OFF_QA = 0
OFF_KA = OFF_QA + D_SWA
OFF_VA = OFF_KA + D_SWA_KV
OFF_QB = OFF_VA + D_SWA_KV
OFF_KB = OFF_QB + D_FOX
OFF_VB = OFF_KB + D_FOX
N_MAIN = OFF_VB + D_FOX


def _layer_norm(z, g, b):
    mu = jnp.mean(z, axis=-1, keepdims=True)
    zc = z - mu
    var = jnp.mean(zc * zc, axis=-1, keepdims=True)
    return zc * lax.rsqrt(var + LN_EPS) * g + b


def _nt_dot(a, b):
    return lax.dot_general(a, b, (((1,), (1,)), ((), ())), preferred_element_type=F32)


def _split3(x):
    hi = x.astype(BF16).astype(F32)
    r = x - hi
    mid = r.astype(BF16).astype(F32)
    lo = (r - mid).astype(BF16).astype(F32)
    return hi, mid, lo


def _mods_kernel(c_ref, w_ref, b_ref, o_ref):
    c = c_ref[...]
    sc = c * jax.nn.sigmoid(c)
    o_ref[...] = jnp.dot(sc, w_ref[...], preferred_element_type=F32,
                         precision=lax.Precision.HIGHEST) + b_ref[...]


def _mods(c, w_ada, b_ada):
    bsz, d = c.shape
    n = w_ada.shape[1]
    tn = 1024
    return pl.pallas_call(
        _mods_kernel,
        grid=(n // tn,),
        in_specs=[pl.BlockSpec((bsz, d), lambda j: (0, 0)),
                  pl.BlockSpec((d, tn), lambda j: (0, j)),
                  pl.BlockSpec((1, tn), lambda j: (0, j))],
        out_specs=pl.BlockSpec((bsz, tn), lambda j: (0, j)),
        out_shape=jax.ShapeDtypeStruct((bsz, n), F32),
        compiler_params=pltpu.CompilerParams(dimension_semantics=("arbitrary",),
                                             vmem_limit_bytes=VMEM_LIMIT),
        name="adaln_mods",
    )(c, w_ada, b_ada.reshape(1, n))


def _ffn_kernel(x_ref, mods_ref, wg_ref, wu_ref, wd_ref, lng_ref, lnb_ref, o_ref,
                h_ref, a_ref, *, mod_base):
    shift = mods_ref[mod_base:mod_base + 1, :]
    scale = mods_ref[mod_base + 1:mod_base + 2, :]
    gate = mods_ref[mod_base + 2:mod_base + 3, :]
    gate_half = (1.0 + gate) * MACARON
    d_ff = wg_ref.shape[1]
    rows = x_ref.shape[0] // FFN_PASSES
    for r0 in range(0, x_ref.shape[0], rows):
        x = x_ref[r0:r0 + rows, :]
        h_ref[r0:r0 + rows, :] = (x * (1.0 + scale) + shift).astype(BF16)
        for c0 in range(0, d_ff, FFN_TF):
            h = h_ref[r0:r0 + rows, :]
            g = jnp.dot(h, wg_ref[:, c0:c0 + FFN_TF], preferred_element_type=F32)
            u = jnp.dot(h, wu_ref[:, c0:c0 + FFN_TF], preferred_element_type=F32)
            a_ref[r0:r0 + rows, c0:c0 + FFN_TF] = (g * jax.nn.sigmoid(g) * u).astype(BF16)
        y = jnp.dot(a_ref[r0:r0 + rows, :], wd_ref[...], preferred_element_type=F32)
        z = ALPHA * x + gate_half * y
        o_ref[r0:r0 + rows, :] = _layer_norm(z, lng_ref[...], lnb_ref[...])


def _ffn(x, mods, wg, wu, wd, ln_g, ln_b, mod_base):
    bsz, seq, d = x.shape
    d_ff = wg.shape[1]
    const = lambda b, i: (0, 0)
    return pl.pallas_call(
        functools.partial(_ffn_kernel, mod_base=mod_base),
        grid=(bsz, seq // FFN_TM),
        in_specs=[pl.BlockSpec((None, FFN_TM, d), lambda b, i: (b, i, 0)),
                  pl.BlockSpec((None, N_MODS, d), lambda b, i: (b, 0, 0)),
                  pl.BlockSpec((d, d_ff), const, pipeline_mode=pl.Buffered(1)),
                  pl.BlockSpec((d, d_ff), const, pipeline_mode=pl.Buffered(1)),
                  pl.BlockSpec((d_ff, d), const, pipeline_mode=pl.Buffered(1)),
                  pl.BlockSpec((1, d), const),
                  pl.BlockSpec((1, d), const)],
        out_specs=pl.BlockSpec((None, FFN_TM, d), lambda b, i: (b, i, 0)),
        out_shape=jax.ShapeDtypeStruct((bsz, seq, d), F32),
        scratch_shapes=[pltpu.VMEM((FFN_TM, d), BF16),
                        pltpu.VMEM((FFN_TM, d_ff), BF16)],
        compiler_params=pltpu.CompilerParams(dimension_semantics=("arbitrary", "arbitrary"),
                                             vmem_limit_bytes=VMEM_LIMIT),
        name=f"ffn_mod{mod_base}",
    )(x, mods, wg, wu, wd, ln_g.reshape(1, d), ln_b.reshape(1, d))


def _inproj_kernel(x_ref, mods_ref, w_ref, wt_ref, wf_ref, bf_ref,
                   qat_ref, ka_ref, vat_ref, qbt_ref, kb_ref, vbt_ref,
                   carry_ref):
    @pl.when(pl.program_id(1) == 0)
    def _():
        carry_ref[...] = jnp.zeros_like(carry_ref)

    x = x_ref[...]
    tm = x.shape[0]
    shift = mods_ref[3:4, :]
    scale = mods_ref[4:5, :]
    h = (x * (1.0 + scale) + shift).astype(BF16)

    kaf = jnp.dot(h, wf_ref[...], preferred_element_type=F32)
    ka = kaf[:, 0:D_SWA_KV]
    f = kaf[:, D_SWA_KV:D_SWA_KV + LANES] + bf_ref[...]
    cum = jnp.minimum(f, 0.0) - jnp.log1p(jnp.exp(-jnp.abs(f)))
    row = lax.broadcasted_iota(jnp.int32, cum.shape, 0)
    k = 1
    while k < tm:
        cum = cum + jnp.where(row >= k, pltpu.roll(cum, k, axis=0), 0.0)
        k *= 2
    cum = cum + carry_ref[...]
    carry_ref[...] = cum[tm - 1:tm, :]
    cum = cum * LOG2_E
    cum_t = cum.T[0:SUBLANES, :]

    qat = (_nt_dot(wt_ref[OFF_QA:OFF_QA + D_SWA, :], h) * (QK_SCALE * LOG2_E)).astype(BF16)
    pw = SWA_PAIR * BLOCK
    for hh in range(SWA_Q_HEADS):
        pair, e = hh // SWA_PAIR, hh % SWA_PAIR
        for blk in range(tm // BLOCK):
            qat_ref[pair * HEAD_DIM:(pair + 1) * HEAD_DIM,
                    blk * pw + e * BLOCK:blk * pw + (e + 1) * BLOCK] = (
                qat[hh * HEAD_DIM:(hh + 1) * HEAD_DIM, blk * BLOCK:(blk + 1) * BLOCK])
    lane = lax.broadcasted_iota(jnp.int32, (tm, LANES), 1)
    krow = lax.broadcasted_iota(jnp.int32, (tm, LANES), 0)
    pos = (krow % BLOCK).astype(F32)
    parity = ((krow // BLOCK) % 2).astype(F32)
    swa_tail = jnp.where(lane < A_ONE, pos,
                         jnp.where(lane < A_PAR, 1.0,
                                   jnp.where(lane < A_PAR + N_SPLIT, parity, 0.0)))
    for g, src in ((0, ka), (1, pltpu.roll(ka, HEAD_DIM, axis=1))):
        ka_ref[:, g * SLOT:(g + 1) * SLOT] = jnp.where(lane < HEAD_DIM, src, swa_tail).astype(BF16)
    vat_ref[...] = _nt_dot(wt_ref[OFF_VA:OFF_VA + D_SWA_KV, :], h).astype(BF16)

    qbt = _nt_dot(wt_ref[OFF_QB:OFF_QB + D_FOX, :], h) * (QK_SCALE * LOG2_E)
    q_hi, q_mid, q_lo = _split3(cum_t)
    sub = lax.broadcasted_iota(jnp.int32, (SUBLANES, tm), 0)
    first_one = K_GATE - HEAD_DIM
    one_row = jnp.where(sub == first_one, 1.0, 0.0)
    for hh in range(FOX_HEADS):
        extra = jnp.where(sub == 0, q_hi[hh:hh + 1, :],
                jnp.where(sub == 1, q_mid[hh:hh + 1, :],
                jnp.where(sub == 2, q_lo[hh:hh + 1, :], one_row)))
        tail = jnp.concatenate(
            [extra] + [one_row] * (N_SPLIT - 1)
            + [jnp.zeros((SLOT - HEAD_DIM - N_SPLIT * SUBLANES, tm), F32)], axis=0)
        qbt_ref[hh * SLOT:hh * SLOT + HEAD_DIM, :] = (
            qbt[hh * HEAD_DIM:(hh + 1) * HEAD_DIM, :].astype(BF16))
        qbt_ref[hh * SLOT + HEAD_DIM:(hh + 1) * SLOT, :] = tail.astype(BF16)

    kb = jnp.dot(h, w_ref[:, OFF_KB:OFF_KB + D_FOX], preferred_element_type=F32)
    k_hi, k_mid, k_lo = _split3(-cum)
    packed = jnp.where(lane < K_STRIDE, k_hi,
             jnp.where(lane < 2 * K_STRIDE, pltpu.roll(k_mid, K_STRIDE, axis=1),
             jnp.where(lane < 3 * K_STRIDE, pltpu.roll(k_lo, 2 * K_STRIDE, axis=1), 0.0)))
    gate_lane = ((lane >= K_GATE) & (lane < K_GATE + N_SPLIT * K_STRIDE)
                 & ((lane - K_GATE) % K_STRIDE == 0))
    ones_lane = jnp.where((lane >= Q_GATE) & (lane < Q_GATE + N_SPLIT), 1.0, 0.0)
    for pair in range(FOX_HEADS // 2):
        kp = kb[:, pair * LANES:(pair + 1) * LANES]
        odd = pltpu.roll(kp, HEAD_DIM, axis=1)
        for e, src in ((0, kp), (1, odd)):
            hh = 2 * pair + e
            gates = pltpu.roll(packed, K_GATE - hh, axis=1)
            tail = jnp.where(gate_lane, gates, ones_lane)
            kb_ref[:, hh * SLOT:(hh + 1) * SLOT] = jnp.where(lane < HEAD_DIM, src, tail).astype(BF16)

    vbt = _nt_dot(wt_ref[OFF_VB:OFF_VB + D_FOX, :], h).astype(BF16)
    ones_row = (lax.broadcasted_iota(jnp.int32, (V_ROWS - HEAD_DIM, ATT_TK), 0) == 0).astype(BF16)
    for r in range(tm // ATT_TK):
        for hh in range(FOX_HEADS):
            vbt_ref[r, hh * V_ROWS:hh * V_ROWS + HEAD_DIM, :] = (
                vbt[hh * HEAD_DIM:(hh + 1) * HEAD_DIM, r * ATT_TK:(r + 1) * ATT_TK])
            vbt_ref[r, hh * V_ROWS + HEAD_DIM:(hh + 1) * V_ROWS, :] = ones_row


def _inproj(x1, mods, w_main, w_main_t, w_f, b_f):
    bsz, seq, d = x1.shape
    tm = PROJ_TM
    const = lambda b, i: (0, 0)
    whole = lambda a: pl.BlockSpec(a.shape, const, pipeline_mode=pl.Buffered(1))
    rows = lambda width: pl.BlockSpec((None, tm, width), lambda b, i: (b, i, 0))
    cols = lambda height, width=tm: pl.BlockSpec((None, height, width), lambda b, i: (b, 0, i))
    return pl.pallas_call(
        _inproj_kernel,
        grid=(bsz, seq // tm),
        in_specs=[rows(d),
                  pl.BlockSpec((None, N_MODS, d), lambda b, i: (b, 0, 0)),
                  whole(w_main), whole(w_main_t), whole(w_f), whole(b_f)],
        out_specs=[cols(D_SWA // SWA_PAIR, SWA_PAIR * tm), rows(D_SWA_SLOTS), cols(D_SWA_KV),
                   cols(D_FOX_SLOTS), rows(D_FOX_SLOTS),
                   pl.BlockSpec((None, tm // ATT_TK, D_FOX_V, ATT_TK), lambda b, i: (b, i, 0, 0))],
        out_shape=[jax.ShapeDtypeStruct((bsz, D_SWA // SWA_PAIR, SWA_PAIR * seq), BF16),
                   jax.ShapeDtypeStruct((bsz, seq, D_SWA_SLOTS), BF16),
                   jax.ShapeDtypeStruct((bsz, D_SWA_KV, seq), BF16),
                   jax.ShapeDtypeStruct((bsz, D_FOX_SLOTS, seq), BF16),
                   jax.ShapeDtypeStruct((bsz, seq, D_FOX_SLOTS), BF16),
                   jax.ShapeDtypeStruct((bsz, seq // ATT_TK, D_FOX_V, ATT_TK), BF16)],
        scratch_shapes=[pltpu.VMEM((1, LANES), F32)],
        compiler_params=pltpu.CompilerParams(dimension_semantics=("arbitrary", "arbitrary"),
                                             vmem_limit_bytes=VMEM_LIMIT),
        name="in_proj",
    )(x1, mods, w_main, w_main_t, w_f, b_f)


def _split3_host(x):
    x = np.asarray(x, np.float32)
    terms = []
    for _ in range(N_SPLIT):
        t = x.astype(jnp.bfloat16).astype(np.float32)
        terms.append(t)
        x = x - t
    return np.stack(terms)


def _alibi_rows():
    rows = np.zeros((SWA_Q_HEADS // SWA_PAIR, 2, HEAD_DIM, SWA_PAIR * BLOCK), np.float32)
    t = np.arange(BLOCK, dtype=np.float32)
    for hh in range(SWA_Q_HEADS):
        pair, e = hh // SWA_PAIR, hh % SWA_PAIR
        c = np.float32(2.0 ** (-8.0 * (hh + 1) / SWA_Q_HEADS) * LOG2_E)
        cols = slice(e * BLOCK, (e + 1) * BLOCK)
        for pq in range(2):
            for base, coeff in ((A_POS, c + 0 * t), (A_ONE, -c * (t + BLOCK * pq)),
                                (A_PAR, -c * BLOCK * (1 - 2 * pq) + 0 * t)):
                rows[pair, pq, base - HEAD_DIM:base - HEAD_DIM + N_SPLIT, cols] = _split3_host(coeff)
    return jnp.asarray(rows, BF16)


def _window_bias():
    kpos = np.arange(2 * BLOCK)[:, None] - BLOCK
    qpos = np.arange(SWA_PAIR * BLOCK)[None, :] % BLOCK
    dist = qpos - kpos
    inside = (dist >= 0) & (dist < WINDOW)
    both = np.stack([inside, inside & (kpos >= 0)])
    return jnp.asarray(np.where(both, 0.0, NEG_BIG), F32)


def _attn_kernel(sinks_ref, x1_ref, mods_ref, qat_ref, kac_ref, kap_ref, vatc_ref, vatp_ref,
                 alibi_ref, winbias_ref, winbias0_ref, qbt_ref, kb_ref, vbt_ref, gain_ref, wout_ref,
                 lng_ref, lnb_ref, o_ref, m_sc, acc_sc, s_sc, ot_sc):
    i = pl.program_id(1)
    tq, tk = ATT_TQ, ATT_TK

    pw = SWA_PAIR * BLOCK
    head_of_lane = lax.broadcasted_iota(jnp.int32, (1, pw), 1) // BLOCK
    ones_rows = (lax.broadcasted_iota(jnp.int32, (V_ROWS - HEAD_DIM, 2 * BLOCK), 0) == 0).astype(BF16)
    units = [(nb, g, c) for nb in range(tq // BLOCK) for g in range(SWA_KV_HEADS)
             for c in range(SWA_GROUP // SWA_PAIR)]

    def swa_scores(nb, g, c):
        r0 = nb * BLOCK
        pair = (g * SWA_GROUP) // SWA_PAIR + c
        k_prev = kap_ref[:, g * SLOT:(g + 1) * SLOT] if nb == 0 else (
            kac_ref[r0 - BLOCK:r0, g * SLOT:(g + 1) * SLOT])
        kk = jnp.concatenate([k_prev, kac_ref[r0:r0 + BLOCK, g * SLOT:(g + 1) * SLOT]], axis=0)
        qop = jnp.concatenate([qat_ref[pair * HEAD_DIM:(pair + 1) * HEAD_DIM, nb * pw:(nb + 1) * pw],
                               alibi_ref[pair, nb % 2]], axis=0)
        return jnp.dot(kk, qop, preferred_element_type=F32)

    def fox_scores(j, hh):
        return jnp.dot(kb_ref[j, :, hh * SLOT:(hh + 1) * SLOT],
                       qbt_ref[hh * SLOT:(hh + 1) * SLOT, :],
                       preferred_element_type=F32)

    for u in range(FOX_LOOKAHEAD):
        s_sc[u] = swa_scores(*units[u])
    for u, (nb, g, c) in enumerate(units):
        s = s_sc[u % FOX_LOOKAHEAD]
        ahead = u + FOX_LOOKAHEAD
        if ahead < len(units):
            s_sc[ahead % FOX_LOOKAHEAD] = swa_scores(*units[ahead])
        else:
            s_sc[ahead % FOX_LOOKAHEAD] = fox_scores(0, ahead - len(units))
        r0 = nb * BLOCK
        h0 = g * SWA_GROUP + c * SWA_PAIR
        s = s + (winbias0_ref[...] if nb == 0 else winbias_ref[...])
        sink = jnp.zeros((1, pw), F32)
        for e in range(SWA_PAIR):
            sink = jnp.where(head_of_lane == e, sinks_ref[h0 + e] * LOG2_E, sink)
        m = jnp.maximum(jnp.max(s, axis=0, keepdims=True), sink)
        p = jnp.exp2(s - m)
        v_prev = vatp_ref[g * HEAD_DIM:(g + 1) * HEAD_DIM, :] if nb == 0 else (
            vatc_ref[g * HEAD_DIM:(g + 1) * HEAD_DIM, r0 - BLOCK:r0])
        vv = jnp.concatenate(
            [v_prev, vatc_ref[g * HEAD_DIM:(g + 1) * HEAD_DIM, r0:r0 + BLOCK]], axis=1)
        pv = jnp.dot(jnp.concatenate([vv, ones_rows], axis=0), p.astype(BF16),
                     preferred_element_type=F32)
        denom = pv[HEAD_DIM:HEAD_DIM + 1, :] + jnp.exp2(sink - m)
        o = pv[0:HEAD_DIM, :] / denom
        for e in range(SWA_PAIR):
            ot_sc[(h0 + e) * HEAD_DIM:(h0 + e + 1) * HEAD_DIM, r0:r0 + BLOCK] = (
                o[:, e * BLOCK:(e + 1) * BLOCK])

    m_sc[...] = jnp.full(m_sc.shape, NEG_BIG, F32)
    acc_sc[...] = jnp.zeros(acc_sc.shape, F32)

    def fox_block(j, causal, has_next):
        pending = {}
        for hh in range(FOX_HEADS):
            s = s_sc[hh] if hh < FOX_LOOKAHEAD else pending.pop(hh)
            ahead = hh + FOX_LOOKAHEAD
            if ahead < FOX_HEADS:
                pending[ahead] = fox_scores(j, ahead)
            elif has_next:
                s_sc[ahead - FOX_HEADS] = fox_scores(j + 1, ahead - FOX_HEADS)
            if causal is not None:
                s = jnp.where(causal, s, NEG_BIG)
            m_prev = m_sc[hh]
            m_new = jnp.maximum(m_prev, jnp.max(s, axis=0, keepdims=True))
            a = jnp.exp2(m_prev - m_new)
            p = jnp.exp2(s - m_new)
            acc_sc[hh] = a * acc_sc[hh] + jnp.dot(
                vbt_ref[j, hh * V_ROWS:(hh + 1) * V_ROWS, :], p.astype(BF16),
                preferred_element_type=F32)
            m_sc[hh] = m_new

    def body(j, carry):
        fox_block(j, None, True)
        return carry

    def body_pair(jj, carry):
        fox_block(2 * jj, None, True)
        fox_block(2 * jj + 1, None, True)
        return carry

    lax.fori_loop(0, i // 2, body_pair, 0)
    lax.fori_loop(2 * (i // 2), i, body, 0)
    kk_i = lax.broadcasted_iota(jnp.int32, (tk, tq), 0)
    qq_i = lax.broadcasted_iota(jnp.int32, (tk, tq), 1)
    fox_block(i, kk_i <= qq_i, False)

    for hh in range(FOX_HEADS):
        ot_sc[D_SWA + hh * HEAD_DIM:D_SWA + (hh + 1) * HEAD_DIM, :] = (
            acc_sc[hh, 0:HEAD_DIM, :] / acc_sc[hh, HEAD_DIM:HEAD_DIM + 1, :])

    o = ot_sc[...].T
    oa = o[:, 0:D_SWA]
    ob = o[:, D_SWA:D_SWA + D_FOX]
    oa = oa * lax.rsqrt(jnp.mean(oa * oa, axis=-1, keepdims=True) + RMS_EPS) * gain_ref[:, 0:D_SWA]
    ob = ob * lax.rsqrt(jnp.mean(ob * ob, axis=-1, keepdims=True) + RMS_EPS) * gain_ref[:, D_SWA:]
    on = jnp.concatenate([oa, ob], axis=-1).astype(BF16)
    y = jnp.dot(on, wout_ref[...], preferred_element_type=F32)
    gate = mods_ref[5:6, :]
    z = ALPHA * x1_ref[...] + (1.0 + gate) * y
    o_ref[...] = _layer_norm(z, lng_ref[...], lnb_ref[...])


def _attention(x1, mods, qat, ka, vat, qbt, kb, vbt, sinks, gain, w_out, ln_g, ln_b):
    bsz, seq, d = x1.shape
    tq = ATT_TQ
    blocks_per_tile = tq // BLOCK
    prev_blk = lambda i: jnp.maximum(i * blocks_per_tile - 1, 0)
    const = lambda b, i, s: (0, 0)
    kb_blocks = kb.reshape(bsz, seq // ATT_TK, ATT_TK, D_FOX_SLOTS)
    alibi = _alibi_rows()
    winbias = _window_bias()
    grid_spec = pltpu.PrefetchScalarGridSpec(
        num_scalar_prefetch=1,
        grid=(bsz, seq // tq),
        in_specs=[pl.BlockSpec((None, tq, d), lambda b, i, s: (b, i, 0)),
                  pl.BlockSpec((None, N_MODS, d), lambda b, i, s: (b, 0, 0)),
                  pl.BlockSpec((None, D_SWA // SWA_PAIR, SWA_PAIR * tq), lambda b, i, s: (b, 0, i)),
                  pl.BlockSpec((None, tq, D_SWA_SLOTS), lambda b, i, s: (b, i, 0)),
                  pl.BlockSpec((None, BLOCK, D_SWA_SLOTS), lambda b, i, s: (b, prev_blk(i), 0)),
                  pl.BlockSpec((None, D_SWA_KV, tq), lambda b, i, s: (b, 0, i)),
                  pl.BlockSpec((None, D_SWA_KV, BLOCK), lambda b, i, s: (b, 0, prev_blk(i))),
                  pl.BlockSpec(alibi.shape, lambda b, i, s: (0, 0, 0, 0)),
                  pl.BlockSpec((None,) + winbias.shape[1:], lambda b, i, s: (0, 0, 0)),
                  pl.BlockSpec((None,) + winbias.shape[1:],
                               lambda b, i, s: (jnp.where(i == 0, 1, 0), 0, 0)),
                  pl.BlockSpec((None, D_FOX_SLOTS, tq), lambda b, i, s: (b, 0, i)),
                  pl.BlockSpec((None, seq // ATT_TK, ATT_TK, D_FOX_SLOTS),
                               lambda b, i, s: (b, 0, 0, 0)),
                  pl.BlockSpec((None, seq // ATT_TK, D_FOX_V, ATT_TK),
                               lambda b, i, s: (b, 0, 0, 0)),
                  pl.BlockSpec((1, d), const),
                  pl.BlockSpec((d, d), const),
                  pl.BlockSpec((1, d), const),
                  pl.BlockSpec((1, d), const)],
        out_specs=pl.BlockSpec((None, tq, d), lambda b, i, s: (b, i, 0)),
        scratch_shapes=[pltpu.VMEM((FOX_HEADS, 1, tq), F32),
                        pltpu.VMEM((FOX_HEADS, V_ROWS, tq), F32),
                        pltpu.VMEM((FOX_LOOKAHEAD, ATT_TK, tq), F32),
                        pltpu.VMEM((d, tq), F32)],
    )
    return pl.pallas_call(
        _attn_kernel,
        grid_spec=grid_spec,
        out_shape=jax.ShapeDtypeStruct((bsz, seq, d), F32),
        compiler_params=pltpu.CompilerParams(dimension_semantics=("arbitrary", "arbitrary"),
                                             vmem_limit_bytes=VMEM_LIMIT),
        name="token_mix",
    )(sinks, x1, mods, qat, ka, ka, vat, vat, alibi, winbias, winbias, qbt, kb_blocks, vbt,
      gain.reshape(1, d), w_out, ln_g.reshape(1, d), ln_b.reshape(1, d))


def kernel(x, c, w_ada, b_ada, ffn1_w_gate, ffn1_w_up, ffn1_w_down, w_in, b_forget,
           swa_sinks, grp_gain, w_out, ffn2_w_gate, ffn2_w_up, ffn2_w_down,
           ln1_g, ln1_b, ln2_g, ln2_b, ln3_g, ln3_b):
    bsz, seq, d = x.shape
    assert w_ada.shape[0] == DEPTH == 1
    assert seq % FFN_TM == 0 and seq % PROJ_TM == 0 and seq % ATT_TQ == 0
    assert PROJ_TM % ATT_TK == 0 and ATT_TQ == ATT_TK and ATT_TQ % BLOCK == 0
    assert w_in.shape[2] == N_MAIN + FOX_HEADS and D_SWA + D_FOX == d
    assert K_STRIDE == SUBLANES and K_GATE + N_SPLIT * K_STRIDE <= SLOT
    assert WINDOW == BLOCK and A_PAR + N_SPLIT <= SLOT
    assert PROJ_TM % (2 * BLOCK) == 0 and ATT_TQ % (2 * BLOCK) == 0
    l = 0
    mods = _mods(c, w_ada[l], b_ada[l]).reshape(bsz, N_MODS, d)

    x1 = _ffn(x, mods, ffn1_w_gate[l].astype(BF16), ffn1_w_up[l].astype(BF16),
              ffn1_w_down[l].astype(BF16), ln1_g[l], ln1_b[l], mod_base=0)

    w_main = w_in[l][:, :N_MAIN].astype(BF16)
    w_f = jnp.concatenate(
        [w_in[l][:, OFF_KA:OFF_KA + D_SWA_KV],
         jnp.pad(w_in[l][:, N_MAIN:], ((0, 0), (0, LANES - FOX_HEADS)))], axis=1).astype(BF16)
    b_f = jnp.pad(b_forget[l], (0, LANES - FOX_HEADS)).reshape(1, LANES)
    qat, ka, vat, qbt, kb, vbt = _inproj(x1, mods, w_main, w_main.T, w_f, b_f)

    x2 = _attention(x1, mods, qat, ka, vat, qbt, kb, vbt, swa_sinks[l],
                    grp_gain[l], w_out[l].astype(BF16), ln2_g[l], ln2_b[l])

    return _ffn(x2, mods, ffn2_w_gate[l].astype(BF16), ffn2_w_up[l].astype(BF16),
                ffn2_w_down[l].astype(BF16), ln3_g[l], ln3_b[l], mod_base=6)
```

```python
import functools
import math

import numpy as np
import jax
import jax.numpy as jnp
from jax import lax
from jax.experimental import pallas as pl
from jax.experimental.pallas import tpu as pltpu

F32 = jnp.float32
BF16 = jnp.bfloat16

SWA_Q_HEADS = 8
SWA_KV_HEADS = 2
SWA_GROUP = SWA_Q_HEADS // SWA_KV_HEADS
FOX_HEADS = 8
HEAD_DIM = 64
D_SWA = SWA_Q_HEADS * HEAD_DIM
D_SWA_KV = SWA_KV_HEADS * HEAD_DIM
D_FOX = FOX_HEADS * HEAD_DIM
WINDOW = 128
BLOCK = 128
MACARON = 0.5
DEPTH = 1
ALPHA = (2.0 * DEPTH) ** 0.25
LN_EPS = 1e-5
RMS_EPS = 1e-6
N_MODS = 9
QK_SCALE = 1.0 / math.sqrt(HEAD_DIM)

LANES = 128
SUBLANES = 8
VMEM_LIMIT = 56 * 1024 * 1024

NEG_BIG = -1e30

FFN_TM = 1024
FFN_TF = 256
FFN_PASSES = 4
PROJ_TM = 512
ATT_TQ = 512
ATT_TK = 256
FOX_LOOKAHEAD = 8

SLOT = LANES
N_SPLIT = 3
Q_GATE = HEAD_DIM
K_GATE = HEAD_DIM + N_SPLIT
K_STRIDE = FOX_HEADS
D_FOX_SLOTS = FOX_HEADS * SLOT
BF16_ROWS = 16
V_ROWS = HEAD_DIM + BF16_ROWS
D_FOX_V = FOX_HEADS * V_ROWS
LOG2_E = math.log2(math.e)
A_POS = HEAD_DIM
A_ONE = A_POS + N_SPLIT
A_PAR = A_ONE + N_SPLIT
D_SWA_SLOTS = SWA_KV_HEADS * SLOT
SWA_PAIR = 2

OFF_QA = 0
OFF_KA = OFF_QA + D_SWA
OFF_VA = OFF_KA + D_SWA_KV
OFF_QB = OFF_VA + D_SWA_KV
OFF_KB = OFF_QB + D_FOX
OFF_VB = OFF_KB + D_FOX
N_MAIN = OFF_VB + D_FOX


def _layer_norm(z, g, b):
    mu = jnp.mean(z, axis=-1, keepdims=True)
    zc = z - mu
    var = jnp.mean(zc * zc, axis=-1, keepdims=True)
    return zc * lax.rsqrt(var + LN_EPS) * g + b


def _nt_dot(a, b):
    return lax.dot_general(a, b, (((1,), (1,)), ((), ())), preferred_element_type=F32)


def _split3(x):
    hi = x.astype(BF16).astype(F32)
    r = x - hi
    mid = r.astype(BF16).astype(F32)
    lo = (r - mid).astype(BF16).astype(F32)
    return hi, mid, lo


def _mods_kernel(c_ref, w_ref, b_ref, o_ref):
    c = c_ref[...]
    sc = c * jax.nn.sigmoid(c)
    o_ref[...] = jnp.dot(sc, w_ref[...], preferred_element_type=F32,
                         precision=lax.Precision.HIGHEST) + b_ref[...]


def _mods(c, w_ada, b_ada):
    bsz, d = c.shape
    n = w_ada.shape[1]
    tn = 1024
    return pl.pallas_call(
        _mods_kernel,
        grid=(n // tn,),
        in_specs=[pl.BlockSpec((bsz, d), lambda j: (0, 0)),
                  pl.BlockSpec((d, tn), lambda j: (0, j)),
                  pl.BlockSpec((1, tn), lambda j: (0, j))],
        out_specs=pl.BlockSpec((bsz, tn), lambda j: (0, j)),
        out_shape=jax.ShapeDtypeStruct((bsz, n), F32),
        compiler_params=pltpu.CompilerParams(dimension_semantics=("arbitrary",),
                                             vmem_limit_bytes=VMEM_LIMIT),
        name="adaln_mods",
    )(c, w_ada, b_ada.reshape(1, n))


def _ffn_kernel(x_ref, mods_ref, wg_ref, wu_ref, wd_ref, lng_ref, lnb_ref, o_ref,
                h_ref, a_ref, *, mod_base):
    shift = mods_ref[mod_base:mod_base + 1, :]
    scale = mods_ref[mod_base + 1:mod_base + 2, :]
    gate = mods_ref[mod_base + 2:mod_base + 3, :]
    gate_half = (1.0 + gate) * MACARON
    d_ff = wg_ref.shape[1]
    rows = x_ref.shape[0] // FFN_PASSES
    for r0 in range(0, x_ref.shape[0], rows):
        x = x_ref[r0:r0 + rows, :]
        h_ref[r0:r0 + rows, :] = (x * (1.0 + scale) + shift).astype(BF16)
        for c0 in range(0, d_ff, FFN_TF):
            h = h_ref[r0:r0 + rows, :]
            g = jnp.dot(h, wg_ref[:, c0:c0 + FFN_TF], preferred_element_type=F32)
            u = jnp.dot(h, wu_ref[:, c0:c0 + FFN_TF], preferred_element_type=F32)
            a_ref[r0:r0 + rows, c0:c0 + FFN_TF] = (g * jax.nn.sigmoid(g) * u).astype(BF16)
        y = jnp.dot(a_ref[r0:r0 + rows, :], wd_ref[...], preferred_element_type=F32)
        z = ALPHA * x + gate_half * y
        o_ref[r0:r0 + rows, :] = _layer_norm(z, lng_ref[...], lnb_ref[...])


def _ffn(x, mods, wg, wu, wd, ln_g, ln_b, mod_base):
    bsz, seq, d = x.shape
    d_ff = wg.shape[1]
    const = lambda b, i: (0, 0)
    return pl.pallas_call(
        functools.partial(_ffn_kernel, mod_base=mod_base),
        grid=(bsz, seq // FFN_TM),
        in_specs=[pl.BlockSpec((None, FFN_TM, d), lambda b, i: (b, i, 0)),
                  pl.BlockSpec((None, N_MODS, d), lambda b, i: (b, 0, 0)),
                  pl.BlockSpec((d, d_ff), const, pipeline_mode=pl.Buffered(1)),
                  pl.BlockSpec((d, d_ff), const, pipeline_mode=pl.Buffered(1)),
                  pl.BlockSpec((d_ff, d), const, pipeline_mode=pl.Buffered(1)),
                  pl.BlockSpec((1, d), const),
                  pl.BlockSpec((1, d), const)],
        out_specs=pl.BlockSpec((None, FFN_TM, d), lambda b, i: (b, i, 0)),
        out_shape=jax.ShapeDtypeStruct((bsz, seq, d), F32),
        scratch_shapes=[pltpu.VMEM((FFN_TM, d), BF16),
                        pltpu.VMEM((FFN_TM, d_ff), BF16)],
        compiler_params=pltpu.CompilerParams(dimension_semantics=("arbitrary", "arbitrary"),
                                             vmem_limit_bytes=VMEM_LIMIT),
        name=f"ffn_mod{mod_base}",
    )(x, mods, wg, wu, wd, ln_g.reshape(1, d), ln_b.reshape(1, d))


def _inproj_kernel(x_ref, mods_ref, w_ref, wt_ref, wf_ref, bf_ref,
                   qat_ref, ka_ref, vat_ref, qbt_ref, kb_ref, vbt_ref,
                   carry_ref):
    @pl.when(pl.program_id(1) == 0)
    def _():
        carry_ref[...] = jnp.zeros_like(carry_ref)

    x = x_ref[...]
    tm = x.shape[0]
    shift = mods_ref[3:4, :]
    scale = mods_ref[4:5, :]
    h = (x * (1.0 + scale) + shift).astype(BF16)

    kaf = jnp.dot(h, wf_ref[...], preferred_element_type=F32)
    ka = kaf[:, 0:D_SWA_KV]
    f = kaf[:, D_SWA_KV:D_SWA_KV + LANES] + bf_ref[...]
    cum = jnp.minimum(f, 0.0) - jnp.log1p(jnp.exp(-jnp.abs(f)))
    row = lax.broadcasted_iota(jnp.int32, cum.shape, 0)
    k = 1
    while k < tm:
        cum = cum + jnp.where(row >= k, pltpu.roll(cum, k, axis=0), 0.0)
        k *= 2
    cum = cum + carry_ref[...]
    carry_ref[...] = cum[tm - 1:tm, :]
    cum = cum * LOG2_E
    cum_t = cum.T[0:SUBLANES, :]

    qat = (_nt_dot(wt_ref[OFF_QA:OFF_QA + D_SWA, :], h) * (QK_SCALE * LOG2_E)).astype(BF16)
    pw = SWA_PAIR * BLOCK
    for hh in range(SWA_Q_HEADS):
        pair, e = hh // SWA_PAIR, hh % SWA_PAIR
        for blk in range(tm // BLOCK):
            qat_ref[pair * HEAD_DIM:(pair + 1) * HEAD_DIM,
                    blk * pw + e * BLOCK:blk * pw + (e + 1) * BLOCK] = (
                qat[hh * HEAD_DIM:(hh + 1) * HEAD_DIM, blk * BLOCK:(blk + 1) * BLOCK])
    lane = lax.broadcasted_iota(jnp.int32, (tm, LANES), 1)
    krow = lax.broadcasted_iota(jnp.int32, (tm, LANES), 0)
    pos = (krow % BLOCK).astype(F32)
    parity = ((krow // BLOCK) % 2).astype(F32)
    swa_tail = jnp.where(lane < A_ONE, pos,
                         jnp.where(lane < A_PAR, 1.0,
                                   jnp.where(lane < A_PAR + N_SPLIT, parity, 0.0)))
    for g, src in ((0, ka), (1, pltpu.roll(ka, HEAD_DIM, axis=1))):
        ka_ref[:, g * SLOT:(g + 1) * SLOT] = jnp.where(lane < HEAD_DIM, src, swa_tail).astype(BF16)
    vat_ref[...] = _nt_dot(wt_ref[OFF_VA:OFF_VA + D_SWA_KV, :], h).astype(BF16)

    qbt = _nt_dot(wt_ref[OFF_QB:OFF_QB + D_FOX, :], h) * (QK_SCALE * LOG2_E)
    q_hi, q_mid, q_lo = _split3(cum_t)
    sub = lax.broadcasted_iota(jnp.int32, (SUBLANES, tm), 0)
    first_one = K_GATE - HEAD_DIM
    one_row = jnp.where(sub == first_one, 1.0, 0.0)
    for hh in range(FOX_HEADS):
        extra = jnp.where(sub == 0, q_hi[hh:hh + 1, :],
                jnp.where(sub == 1, q_mid[hh:hh + 1, :],
                jnp.where(sub == 2, q_lo[hh:hh + 1, :], one_row)))
        tail = jnp.concatenate(
            [extra] + [one_row] * (N_SPLIT - 1)
            + [jnp.zeros((SLOT - HEAD_DIM - N_SPLIT * SUBLANES, tm), F32)], axis=0)
        qbt_ref[hh * SLOT:hh * SLOT + HEAD_DIM, :] = (
            qbt[hh * HEAD_DIM:(hh + 1) * HEAD_DIM, :].astype(BF16))
        qbt_ref[hh * SLOT + HEAD_DIM:(hh + 1) * SLOT, :] = tail.astype(BF16)

    kb = jnp.dot(h, w_ref[:, OFF_KB:OFF_KB + D_FOX], preferred_element_type=F32)
    k_hi, k_mid, k_lo = _split3(-cum)
    packed = jnp.where(lane < K_STRIDE, k_hi,
             jnp.where(lane < 2 * K_STRIDE, pltpu.roll(k_mid, K_STRIDE, axis=1),
             jnp.where(lane < 3 * K_STRIDE, pltpu.roll(k_lo, 2 * K_STRIDE, axis=1), 0.0)))
    gate_lane = ((lane >= K_GATE) & (lane < K_GATE + N_SPLIT * K_STRIDE)
                 & ((lane - K_GATE) % K_STRIDE == 0))
    ones_lane = jnp.where((lane >= Q_GATE) & (lane < Q_GATE + N_SPLIT), 1.0, 0.0)
    for pair in range(FOX_HEADS // 2):
        kp = kb[:, pair * LANES:(pair + 1) * LANES]
        odd = pltpu.roll(kp, HEAD_DIM, axis=1)
        for e, src in ((0, kp), (1, odd)):
            hh = 2 * pair + e
            gates = pltpu.roll(packed, K_GATE - hh, axis=1)
            tail = jnp.where(gate_lane, gates, ones_lane)
            kb_ref[:, hh * SLOT:(hh + 1) * SLOT] = jnp.where(lane < HEAD_DIM, src, tail).astype(BF16)

    vbt = _nt_dot(wt_ref[OFF_VB:OFF_VB + D_FOX, :], h).astype(BF16)
    ones_row = (lax.broadcasted_iota(jnp.int32, (V_ROWS - HEAD_DIM, ATT_TK), 0) == 0).astype(BF16)
    for r in range(tm // ATT_TK):
        for hh in range(FOX_HEADS):
            vbt_ref[r, hh * V_ROWS:hh * V_ROWS + HEAD_DIM, :] = (
                vbt[hh * HEAD_DIM:(hh + 1) * HEAD_DIM, r * ATT_TK:(r + 1) * ATT_TK])
            vbt_ref[r, hh * V_ROWS + HEAD_DIM:(hh + 1) * V_ROWS, :] = ones_row


def _inproj(x1, mods, w_main, w_main_t, w_f, b_f):
    bsz, seq, d = x1.shape
    tm = PROJ_TM
    const = lambda b, i: (0, 0)
    whole = lambda a: pl.BlockSpec(a.shape, const, pipeline_mode=pl.Buffered(1))
    rows = lambda width: pl.BlockSpec((None, tm, width), lambda b, i: (b, i, 0))
    cols = lambda height, width=tm: pl.BlockSpec((None, height, width), lambda b, i: (b, 0, i))
    return pl.pallas_call(
        _inproj_kernel,
        grid=(bsz, seq // tm),
        in_specs=[rows(d),
                  pl.BlockSpec((None, N_MODS, d), lambda b, i: (b, 0, 0)),
                  whole(w_main), whole(w_main_t), whole(w_f), whole(b_f)],
        out_specs=[cols(D_SWA // SWA_PAIR, SWA_PAIR * tm), rows(D_SWA_SLOTS), cols(D_SWA_KV),
                   cols(D_FOX_SLOTS), rows(D_FOX_SLOTS),
                   pl.BlockSpec((None, tm // ATT_TK, D_FOX_V, ATT_TK), lambda b, i: (b, i, 0, 0))],
        out_shape=[jax.ShapeDtypeStruct((bsz, D_SWA // SWA_PAIR, SWA_PAIR * seq), BF16),
                   jax.ShapeDtypeStruct((bsz, seq, D_SWA_SLOTS), BF16),
                   jax.ShapeDtypeStruct((bsz, D_SWA_KV, seq), BF16),
                   jax.ShapeDtypeStruct((bsz, D_FOX_SLOTS, seq), BF16),
                   jax.ShapeDtypeStruct((bsz, seq, D_FOX_SLOTS), BF16),
                   jax.ShapeDtypeStruct((bsz, seq // ATT_TK, D_FOX_V, ATT_TK), BF16)],
        scratch_shapes=[pltpu.VMEM((1, LANES), F32)],
        compiler_params=pltpu.CompilerParams(dimension_semantics=("arbitrary", "arbitrary"),
                                             vmem_limit_bytes=VMEM_LIMIT),
        name="in_proj",
    )(x1, mods, w_main, w_main_t, w_f, b_f)


def _split3_host(x):
    x = np.asarray(x, np.float32)
    terms = []
    for _ in range(N_SPLIT):
        t = x.astype(jnp.bfloat16).astype(np.float32)
        terms.append(t)
        x = x - t
    return np.stack(terms)


def _alibi_rows():
    rows = np.zeros((SWA_Q_HEADS // SWA_PAIR, 2, HEAD_DIM, SWA_PAIR * BLOCK), np.float32)
    t = np.arange(BLOCK, dtype=np.float32)
    for hh in range(SWA_Q_HEADS):
        pair, e = hh // SWA_PAIR, hh % SWA_PAIR
        c = np.float32(2.0 ** (-8.0 * (hh + 1) / SWA_Q_HEADS) * LOG2_E)
        cols = slice(e * BLOCK, (e + 1) * BLOCK)
        for pq in range(2):
            for base, coeff in ((A_POS, c + 0 * t), (A_ONE, -c * (t + BLOCK * pq)),
                                (A_PAR, -c * BLOCK * (1 - 2 * pq) + 0 * t)):
                rows[pair, pq, base - HEAD_DIM:base - HEAD_DIM + N_SPLIT, cols] = _split3_host(coeff)
    return jnp.asarray(rows, BF16)


def _window_bias():
    kpos = np.arange(2 * BLOCK)[:, None] - BLOCK
    qpos = np.arange(SWA_PAIR * BLOCK)[None, :] % BLOCK
    dist = qpos - kpos
    inside = (dist >= 0) & (dist < WINDOW)
    both = np.stack([inside, inside & (kpos >= 0)])
    return jnp.asarray(np.where(both, 0.0, NEG_BIG), F32)


def _attn_kernel(sinks_ref, x1_ref, mods_ref, qat_ref, kac_ref, kap_ref, vatc_ref, vatp_ref,
                 alibi_ref, winbias_ref, winbias0_ref, qbt_ref, kb_ref, vbt_ref, gain_ref, wout_ref,
                 lng_ref, lnb_ref, o_ref, m_sc, acc_sc, s_sc, ot_sc):
    i = pl.program_id(1)
    tq, tk = ATT_TQ, ATT_TK

    pw = SWA_PAIR * BLOCK
    head_of_lane = lax.broadcasted_iota(jnp.int32, (1, pw), 1) // BLOCK
    ones_rows = (lax.broadcasted_iota(jnp.int32, (V_ROWS - HEAD_DIM, 2 * BLOCK), 0) == 0).astype(BF16)
    units = [(nb, g, c) for nb in range(tq // BLOCK) for g in range(SWA_KV_HEADS)
             for c in range(SWA_GROUP // SWA_PAIR)]

    def swa_scores(nb, g, c):
        r0 = nb * BLOCK
        pair = (g * SWA_GROUP) // SWA_PAIR + c
        k_prev = kap_ref[:, g * SLOT:(g + 1) * SLOT] if nb == 0 else (
            kac_ref[r0 - BLOCK:r0, g * SLOT:(g + 1) * SLOT])
        kk = jnp.concatenate([k_prev, kac_ref[r0:r0 + BLOCK, g * SLOT:(g + 1) * SLOT]], axis=0)
        qop = jnp.concatenate([qat_ref[pair * HEAD_DIM:(pair + 1) * HEAD_DIM, nb * pw:(nb + 1) * pw],
                               alibi_ref[pair, nb % 2]], axis=0)
        return jnp.dot(kk, qop, preferred_element_type=F32)

    def fox_scores(j, hh):
        return jnp.dot(kb_ref[j, :, hh * SLOT:(hh + 1) * SLOT],
                       qbt_ref[hh * SLOT:(hh + 1) * SLOT, :],
                       preferred_element_type=F32)

    for u in range(FOX_LOOKAHEAD):
        s_sc[u, :, 0:pw] = swa_scores(*units[u])
    for u, (nb, g, c) in enumerate(units):
        s = s_sc[u % FOX_LOOKAHEAD, :, 0:pw]
        ahead = u + FOX_LOOKAHEAD
        if ahead < len(units):
            s_sc[ahead % FOX_LOOKAHEAD, :, 0:pw] = swa_scores(*units[ahead])
        else:
            s_sc[ahead % FOX_LOOKAHEAD] = fox_scores(0, ahead - len(units))
        r0 = nb * BLOCK
        h0 = g * SWA_GROUP + c * SWA_PAIR
        s = s + (winbias0_ref[...] if nb == 0 else winbias_ref[...])
        sink = jnp.zeros((1, pw), F32)
        for e in range(SWA_PAIR):
            sink = jnp.where(head_of_lane == e, sinks_ref[h0 + e] * LOG2_E, sink)
        m = jnp.maximum(jnp.max(s, axis=0, keepdims=True), sink)
        p = jnp.exp2(s - m)
        v_prev = vatp_ref[g * HEAD_DIM:(g + 1) * HEAD_DIM, :] if nb == 0 else (
            vatc_ref[g * HEAD_DIM:(g + 1) * HEAD_DIM, r0 - BLOCK:r0])
        vv = jnp.concatenate(
            [v_prev, vatc_ref[g * HEAD_DIM:(g + 1) * HEAD_DIM, r0:r0 + BLOCK]], axis=1)
        pv = jnp.dot(jnp.concatenate([vv, ones_rows], axis=0), p.astype(BF16),
                     preferred_element_type=F32)
        denom = pv[HEAD_DIM:HEAD_DIM + 1, :] + jnp.exp2(sink - m)
        o = pv[0:HEAD_DIM, :] / denom
        for e in range(SWA_PAIR):
            ot_sc[(h0 + e) * HEAD_DIM:(h0 + e + 1) * HEAD_DIM, r0:r0 + BLOCK] = (
                o[:, e * BLOCK:(e + 1) * BLOCK])

    m_sc[...] = jnp.full(m_sc.shape, NEG_BIG, F32)
    acc_sc[...] = jnp.zeros(acc_sc.shape, F32)

    def fox_block(j, causal, has_next):
        pending = {}
        for hh in range(FOX_HEADS):
            s = s_sc[hh] if hh < FOX_LOOKAHEAD else pending.pop(hh)
            ahead = hh + FOX_LOOKAHEAD
            if ahead < FOX_HEADS:
                pending[ahead] = fox_scores(j, ahead)
            elif has_next:
                s_sc[ahead - FOX_HEADS] = fox_scores(j + 1, ahead - FOX_HEADS)
            if causal is not None:
                s = jnp.where(causal, s, NEG_BIG)
            m_prev = m_sc[hh]
            m_new = jnp.maximum(m_prev, jnp.max(s, axis=0, keepdims=True))
            a = jnp.exp2(m_prev - m_new)
            p = jnp.exp2(s - m_new)
            acc_sc[hh] = a * acc_sc[hh] + jnp.dot(
                vbt_ref[j, hh * V_ROWS:(hh + 1) * V_ROWS, :], p.astype(BF16),
                preferred_element_type=F32)
            m_sc[hh] = m_new

    per_tile = tq // tk

    def body(jj, carry):
        for r in range(per_tile):
            fox_block(per_tile * jj + r, None, True)
        return carry

    lax.fori_loop(0, i, body, 0)
    kk_i = lax.broadcasted_iota(jnp.int32, (tk, tq), 0)
    qq_i = lax.broadcasted_iota(jnp.int32, (tk, tq), 1)
    for r in range(per_tile):
        fox_block(per_tile * i + r, kk_i + r * tk <= qq_i, r + 1 < per_tile)

    for hh in range(FOX_HEADS):
        ot_sc[D_SWA + hh * HEAD_DIM:D_SWA + (hh + 1) * HEAD_DIM, :] = (
            acc_sc[hh, 0:HEAD_DIM, :] / acc_sc[hh, HEAD_DIM:HEAD_DIM + 1, :])

    o = ot_sc[...].T
    oa = o[:, 0:D_SWA]
    ob = o[:, D_SWA:D_SWA + D_FOX]
    oa = oa * lax.rsqrt(jnp.mean(oa * oa, axis=-1, keepdims=True) + RMS_EPS) * gain_ref[:, 0:D_SWA]
    ob = ob * lax.rsqrt(jnp.mean(ob * ob, axis=-1, keepdims=True) + RMS_EPS) * gain_ref[:, D_SWA:]
    on = jnp.concatenate([oa, ob], axis=-1).astype(BF16)
    y = jnp.dot(on, wout_ref[...], preferred_element_type=F32)
    gate = mods_ref[5:6, :]
    z = ALPHA * x1_ref[...] + (1.0 + gate) * y
    o_ref[...] = _layer_norm(z, lng_ref[...], lnb_ref[...])


def _attention(x1, mods, qat, ka, vat, qbt, kb, vbt, sinks, gain, w_out, ln_g, ln_b):
    bsz, seq, d = x1.shape
    tq = ATT_TQ
    blocks_per_tile = tq // BLOCK
    prev_blk = lambda i: jnp.maximum(i * blocks_per_tile - 1, 0)
    const = lambda b, i, s: (0, 0)
    kb_blocks = kb.reshape(bsz, seq // ATT_TK, ATT_TK, D_FOX_SLOTS)
    alibi = _alibi_rows()
    winbias = _window_bias()
    grid_spec = pltpu.PrefetchScalarGridSpec(
        num_scalar_prefetch=1,
        grid=(bsz, seq // tq),
        in_specs=[pl.BlockSpec((None, tq, d), lambda b, i, s: (b, i, 0)),
                  pl.BlockSpec((None, N_MODS, d), lambda b, i, s: (b, 0, 0)),
                  pl.BlockSpec((None, D_SWA // SWA_PAIR, SWA_PAIR * tq), lambda b, i, s: (b, 0, i)),
                  pl.BlockSpec((None, tq, D_SWA_SLOTS), lambda b, i, s: (b, i, 0)),
                  pl.BlockSpec((None, BLOCK, D_SWA_SLOTS), lambda b, i, s: (b, prev_blk(i), 0)),
                  pl.BlockSpec((None, D_SWA_KV, tq), lambda b, i, s: (b, 0, i)),
                  pl.BlockSpec((None, D_SWA_KV, BLOCK), lambda b, i, s: (b, 0, prev_blk(i))),
                  pl.BlockSpec(alibi.shape, lambda b, i, s: (0, 0, 0, 0)),
                  pl.BlockSpec((None,) + winbias.shape[1:], lambda b, i, s: (0, 0, 0)),
                  pl.BlockSpec((None,) + winbias.shape[1:],
                               lambda b, i, s: (jnp.where(i == 0, 1, 0), 0, 0)),
                  pl.BlockSpec((None, D_FOX_SLOTS, tq), lambda b, i, s: (b, 0, i)),
                  pl.BlockSpec((None, seq // ATT_TK, ATT_TK, D_FOX_SLOTS),
                               lambda b, i, s: (b, 0, 0, 0)),
                  pl.BlockSpec((None, seq // ATT_TK, D_FOX_V, ATT_TK),
                               lambda b, i, s: (b, 0, 0, 0)),
                  pl.BlockSpec((1, d), const),
                  pl.BlockSpec((d, d), const),
                  pl.BlockSpec((1, d), const),
                  pl.BlockSpec((1, d), const)],
        out_specs=pl.BlockSpec((None, tq, d), lambda b, i, s: (b, i, 0)),
        scratch_shapes=[pltpu.VMEM((FOX_HEADS, 1, tq), F32),
                        pltpu.VMEM((FOX_HEADS, V_ROWS, tq), F32),
                        pltpu.VMEM((FOX_LOOKAHEAD, ATT_TK, tq), F32),
                        pltpu.VMEM((d, tq), F32)],
    )
    return pl.pallas_call(
        _attn_kernel,
        grid_spec=grid_spec,
        out_shape=jax.ShapeDtypeStruct((bsz, seq, d), F32),
        compiler_params=pltpu.CompilerParams(dimension_semantics=("arbitrary", "arbitrary"),
                                             vmem_limit_bytes=VMEM_LIMIT),
        name="token_mix",
    )(sinks, x1, mods, qat, ka, ka, vat, vat, alibi, winbias, winbias, qbt, kb_blocks, vbt,
      gain.reshape(1, d), w_out, ln_g.reshape(1, d), ln_b.reshape(1, d))


def kernel(x, c, w_ada, b_ada, ffn1_w_gate, ffn1_w_up, ffn1_w_down, w_in, b_forget,
           swa_sinks, grp_gain, w_out, ffn2_w_gate, ffn2_w_up, ffn2_w_down,
           ln1_g, ln1_b, ln2_g, ln2_b, ln3_g, ln3_b):
    bsz, seq, d = x.shape
    assert w_ada.shape[0] == DEPTH == 1
    assert seq % FFN_TM == 0 and seq % PROJ_TM == 0 and seq % ATT_TQ == 0
    assert PROJ_TM % ATT_TK == 0 and ATT_TQ % ATT_TK == 0 and SWA_PAIR * BLOCK <= ATT_TQ
    assert w_in.shape[2] == N_MAIN + FOX_HEADS and D_SWA + D_FOX == d
    assert K_STRIDE == SUBLANES and K_GATE + N_SPLIT * K_STRIDE <= SLOT
    assert WINDOW == BLOCK and A_PAR + N_SPLIT <= SLOT
    assert PROJ_TM % (2 * BLOCK) == 0 and ATT_TQ % (2 * BLOCK) == 0
    l = 0
    mods = _mods(c, w_ada[l], b_ada[l]).reshape(bsz, N_MODS, d)

    x1 = _ffn(x, mods, ffn1_w_gate[l].astype(BF16), ffn1_w_up[l].astype(BF16),
              ffn1_w_down[l].astype(BF16), ln1_g[l], ln1_b[l], mod_base=0)

    w_main = w_in[l][:, :N_MAIN].astype(BF16)
    w_f = jnp.concatenate(
        [w_in[l][:, OFF_KA:OFF_KA + D_SWA_KV],
         jnp.pad(w_in[l][:, N_MAIN:], ((0, 0), (0, LANES - FOX_HEADS)))], axis=1).astype(BF16)
    b_f = jnp.pad(b_forget[l], (0, LANES - FOX_HEADS)).reshape(1, LANES)
    qat, ka, vat, qbt, kb, vbt = _inproj(x1, mods, w_main, w_main.T, w_f, b_f)

    x2 = _attention(x1, mods, qat, ka, vat, qbt, kb, vbt, swa_sinks[l],
                    grp_gain[l], w_out[l].astype(BF16), ln2_g[l], ln2_b[l])

    return _ffn(x2, mods, ffn2_w_gate[l].astype(BF16), ffn2_w_up[l].astype(BF16),
                ffn2_w_down[l].astype(BF16), ln3_g[l], ln3_b[l], mod_base=6)
```

```python
import functools
import math

import numpy as np
import jax
import jax.numpy as jnp
from jax import lax
from jax.experimental import pallas as pl
from jax.experimental.pallas import tpu as pltpu

F32 = jnp.float32
BF16 = jnp.bfloat16

SWA_Q_HEADS = 8
SWA_KV_HEADS = 2
SWA_GROUP = SWA_Q_HEADS // SWA_KV_HEADS
FOX_HEADS = 8
HEAD_DIM = 64
D_SWA = SWA_Q_HEADS * HEAD_DIM
D_SWA_KV = SWA_KV_HEADS * HEAD_DIM
D_FOX = FOX_HEADS * HEAD_DIM
WINDOW = 128
BLOCK = 128
MACARON = 0.5
DEPTH = 1
ALPHA = (2.0 * DEPTH) ** 0.25
LN_EPS = 1e-5
RMS_EPS = 1e-6
N_MODS = 9
QK_SCALE = 1.0 / math.sqrt(HEAD_DIM)

LANES = 128
SUBLANES = 8
VMEM_LIMIT = 56 * 1024 * 1024

NEG_BIG = -1e30

FFN_TM = 1024
FFN_TF = 256
FFN_PASSES = 4
PROJ_TM = 512
ATT_TQ = 512
ATT_TK = 256
FOX_LOOKAHEAD = 8

SLOT = LANES
N_SPLIT = 3
Q_GATE = HEAD_DIM
K_GATE = HEAD_DIM + N_SPLIT
K_STRIDE = FOX_HEADS
D_FOX_SLOTS = FOX_HEADS * SLOT
BF16_ROWS = 16
V_ROWS = HEAD_DIM + BF16_ROWS
D_FOX_V = FOX_HEADS * V_ROWS
LOG2_E = math.log2(math.e)
A_POS = HEAD_DIM
A_ONE = A_POS + N_SPLIT
A_PAR = A_ONE + N_SPLIT
D_SWA_SLOTS = SWA_KV_HEADS * SLOT
SWA_PAIR = 2

OFF_QA = 0
OFF_KA = OFF_QA + D_SWA
OFF_VA = OFF_KA + D_SWA_KV
OFF_QB = OFF_VA + D_SWA_KV
OFF_KB = OFF_QB + D_FOX
OFF_VB = OFF_KB + D_FOX
N_MAIN = OFF_VB + D_FOX


def _layer_norm(z, g, b):
    mu = jnp.mean(z, axis=-1, keepdims=True)
    zc = z - mu
    var = jnp.mean(zc * zc, axis=-1, keepdims=True)
    return zc * lax.rsqrt(var + LN_EPS) * g + b


def _nt_dot(a, b):
    return lax.dot_general(a, b, (((1,), (1,)), ((), ())), preferred_element_type=F32)


def _split3(x):
    hi = x.astype(BF16).astype(F32)
    r = x - hi
    mid = r.astype(BF16).astype(F32)
    lo = (r - mid).astype(BF16).astype(F32)
    return hi, mid, lo


def _mods_kernel(c_ref, w_ref, b_ref, o_ref):
    c = c_ref[...]
    sc = c * jax.nn.sigmoid(c)
    o_ref[...] = jnp.dot(sc, w_ref[...], preferred_element_type=F32,
                         precision=lax.Precision.HIGHEST) + b_ref[...]


def _mods(c, w_ada, b_ada):
    bsz, d = c.shape
    n = w_ada.shape[1]
    tn = 1024
    return pl.pallas_call(
        _mods_kernel,
        grid=(n // tn,),
        in_specs=[pl.BlockSpec((bsz, d), lambda j: (0, 0)),
                  pl.BlockSpec((d, tn), lambda j: (0, j)),
                  pl.BlockSpec((1, tn), lambda j: (0, j))],
        out_specs=pl.BlockSpec((bsz, tn), lambda j: (0, j)),
        out_shape=jax.ShapeDtypeStruct((bsz, n), F32),
        compiler_params=pltpu.CompilerParams(dimension_semantics=("arbitrary",),
                                             vmem_limit_bytes=VMEM_LIMIT),
        name="adaln_mods",
    )(c, w_ada, b_ada.reshape(1, n))


def _ffn_kernel(x_ref, mods_ref, wg_ref, wu_ref, wd_ref, lng_ref, lnb_ref, o_ref,
                h_ref, a_ref, *, mod_base):
    shift = mods_ref[mod_base:mod_base + 1, :]
    scale = mods_ref[mod_base + 1:mod_base + 2, :]
    gate = mods_ref[mod_base + 2:mod_base + 3, :]
    gate_half = (1.0 + gate) * MACARON
    d_ff = wg_ref.shape[1]
    rows = x_ref.shape[0] // FFN_PASSES
    for r0 in range(0, x_ref.shape[0], rows):
        x = x_ref[r0:r0 + rows, :]
        h_ref[r0:r0 + rows, :] = (x * (1.0 + scale) + shift).astype(BF16)
        for c0 in range(0, d_ff, FFN_TF):
            h = h_ref[r0:r0 + rows, :]
            g = jnp.dot(h, wg_ref[:, c0:c0 + FFN_TF], preferred_element_type=F32)
            u = jnp.dot(h, wu_ref[:, c0:c0 + FFN_TF], preferred_element_type=F32)
            a_ref[r0:r0 + rows, c0:c0 + FFN_TF] = (g * jax.nn.sigmoid(g) * u).astype(BF16)
        y = jnp.dot(a_ref[r0:r0 + rows, :], wd_ref[...], preferred_element_type=F32)
        z = ALPHA * x + gate_half * y
        o_ref[r0:r0 + rows, :] = _layer_norm(z, lng_ref[...], lnb_ref[...])


def _ffn(x, mods, wg, wu, wd, ln_g, ln_b, mod_base):
    bsz, seq, d = x.shape
    d_ff = wg.shape[1]
    const = lambda b, i: (0, 0)
    return pl.pallas_call(
        functools.partial(_ffn_kernel, mod_base=mod_base),
        grid=(bsz, seq // FFN_TM),
        in_specs=[pl.BlockSpec((None, FFN_TM, d), lambda b, i: (b, i, 0)),
                  pl.BlockSpec((None, N_MODS, d), lambda b, i: (b, 0, 0)),
                  pl.BlockSpec((d, d_ff), const, pipeline_mode=pl.Buffered(1)),
                  pl.BlockSpec((d, d_ff), const, pipeline_mode=pl.Buffered(1)),
                  pl.BlockSpec((d_ff, d), const, pipeline_mode=pl.Buffered(1)),
                  pl.BlockSpec((1, d), const),
                  pl.BlockSpec((1, d), const)],
        out_specs=pl.BlockSpec((None, FFN_TM, d), lambda b, i: (b, i, 0)),
        out_shape=jax.ShapeDtypeStruct((bsz, seq, d), F32),
        scratch_shapes=[pltpu.VMEM((FFN_TM, d), BF16),
                        pltpu.VMEM((FFN_TM, d_ff), BF16)],
        compiler_params=pltpu.CompilerParams(dimension_semantics=("arbitrary", "arbitrary"),
                                             vmem_limit_bytes=VMEM_LIMIT),
        name=f"ffn_mod{mod_base}",
    )(x, mods, wg, wu, wd, ln_g.reshape(1, d), ln_b.reshape(1, d))


def _inproj_kernel(x_ref, mods_ref, w_ref, wt_ref, wf_ref, bf_ref,
                   qat_ref, ka_ref, vat_ref, qbt_ref, kb_ref, vbt_ref,
                   carry_ref):
    @pl.when(pl.program_id(1) == 0)
    def _():
        carry_ref[...] = jnp.zeros_like(carry_ref)

    x = x_ref[...]
    tm = x.shape[0]
    shift = mods_ref[3:4, :]
    scale = mods_ref[4:5, :]
    h = (x * (1.0 + scale) + shift).astype(BF16)

    kaf = jnp.dot(h, wf_ref[...], preferred_element_type=F32)
    ka = kaf[:, 0:D_SWA_KV]
    f = kaf[:, D_SWA_KV:D_SWA_KV + LANES] + bf_ref[...]
    f_t = f.T[0:SUBLANES, :]
    cum_t = jnp.minimum(f_t, 0.0) - jnp.log1p(jnp.exp(-jnp.abs(f_t)))
    pos_t = lax.broadcasted_iota(jnp.int32, cum_t.shape, 1)
    k = 1
    while k < tm:
        cum_t = cum_t + jnp.where(pos_t >= k, pltpu.roll(cum_t, k, axis=1), 0.0)
        k *= 2
    cum_t = cum_t + carry_ref[...]
    carry_ref[...] = cum_t[:, tm - 1:tm]
    cum_t = cum_t * LOG2_E
    cum = jnp.concatenate(
        [cum_t, jnp.zeros((LANES - SUBLANES, tm), F32)], axis=0).T

    qat = (_nt_dot(wt_ref[OFF_QA:OFF_QA + D_SWA, :], h) * (QK_SCALE * LOG2_E)).astype(BF16)
    pw = SWA_PAIR * BLOCK
    for hh in range(SWA_Q_HEADS):
        pair, e = hh // SWA_PAIR, hh % SWA_PAIR
        for blk in range(tm // BLOCK):
            qat_ref[pair * HEAD_DIM:(pair + 1) * HEAD_DIM,
                    blk * pw + e * BLOCK:blk * pw + (e + 1) * BLOCK] = (
                qat[hh * HEAD_DIM:(hh + 1) * HEAD_DIM, blk * BLOCK:(blk + 1) * BLOCK])
    lane = lax.broadcasted_iota(jnp.int32, (tm, LANES), 1)
    krow = lax.broadcasted_iota(jnp.int32, (tm, LANES), 0)
    pos = (krow % BLOCK).astype(F32)
    parity = ((krow // BLOCK) % 2).astype(F32)
    swa_tail = jnp.where(lane < A_ONE, pos,
                         jnp.where(lane < A_PAR, 1.0,
                                   jnp.where(lane < A_PAR + N_SPLIT, parity, 0.0)))
    for g, src in ((0, ka), (1, pltpu.roll(ka, HEAD_DIM, axis=1))):
        ka_ref[:, g * SLOT:(g + 1) * SLOT] = jnp.where(lane < HEAD_DIM, src, swa_tail).astype(BF16)
    vat_ref[...] = _nt_dot(wt_ref[OFF_VA:OFF_VA + D_SWA_KV, :], h).astype(BF16)

    qbt = _nt_dot(wt_ref[OFF_QB:OFF_QB + D_FOX, :], h) * (QK_SCALE * LOG2_E)
    q_hi, q_mid, q_lo = _split3(cum_t)
    sub = lax.broadcasted_iota(jnp.int32, (SUBLANES, tm), 0)
    first_one = K_GATE - HEAD_DIM
    one_row = jnp.where(sub == first_one, 1.0, 0.0)
    for hh in range(FOX_HEADS):
        extra = jnp.where(sub == 0, q_hi[hh:hh + 1, :],
                jnp.where(sub == 1, q_mid[hh:hh + 1, :],
                jnp.where(sub == 2, q_lo[hh:hh + 1, :], one_row)))
        tail = jnp.concatenate(
            [extra] + [one_row] * (N_SPLIT - 1)
            + [jnp.zeros((SLOT - HEAD_DIM - N_SPLIT * SUBLANES, tm), F32)], axis=0)
        qbt_ref[hh * SLOT:hh * SLOT + HEAD_DIM, :] = (
            qbt[hh * HEAD_DIM:(hh + 1) * HEAD_DIM, :].astype(BF16))
        qbt_ref[hh * SLOT + HEAD_DIM:(hh + 1) * SLOT, :] = tail.astype(BF16)

    kb = jnp.dot(h, w_ref[:, OFF_KB:OFF_KB + D_FOX], preferred_element_type=F32)
    k_hi, k_mid, k_lo = _split3(-cum)
    packed = jnp.where(lane < K_STRIDE, k_hi,
             jnp.where(lane < 2 * K_STRIDE, pltpu.roll(k_mid, K_STRIDE, axis=1),
             jnp.where(lane < 3 * K_STRIDE, pltpu.roll(k_lo, 2 * K_STRIDE, axis=1), 0.0)))
    gate_lane = ((lane >= K_GATE) & (lane < K_GATE + N_SPLIT * K_STRIDE)
                 & ((lane - K_GATE) % K_STRIDE == 0))
    ones_lane = jnp.where((lane >= Q_GATE) & (lane < Q_GATE + N_SPLIT), 1.0, 0.0)
    for pair in range(FOX_HEADS // 2):
        kp = kb[:, pair * LANES:(pair + 1) * LANES]
        odd = pltpu.roll(kp, HEAD_DIM, axis=1)
        for e, src in ((0, kp), (1, odd)):
            hh = 2 * pair + e
            gates = pltpu.roll(packed, K_GATE - hh, axis=1)
            tail = jnp.where(gate_lane, gates, ones_lane)
            kb_ref[:, hh * SLOT:(hh + 1) * SLOT] = jnp.where(lane < HEAD_DIM, src, tail).astype(BF16)

    vbt = _nt_dot(wt_ref[OFF_VB:OFF_VB + D_FOX, :], h).astype(BF16)
    ones_row = (lax.broadcasted_iota(jnp.int32, (V_ROWS - HEAD_DIM, ATT_TK), 0) == 0).astype(BF16)
    for r in range(tm // ATT_TK):
        for hh in range(FOX_HEADS):
            vbt_ref[r, hh * V_ROWS:hh * V_ROWS + HEAD_DIM, :] = (
                vbt[hh * HEAD_DIM:(hh + 1) * HEAD_DIM, r * ATT_TK:(r + 1) * ATT_TK])
            vbt_ref[r, hh * V_ROWS + HEAD_DIM:(hh + 1) * V_ROWS, :] = ones_row


def _inproj(x1, mods, w_main, w_main_t, w_f, b_f):
    bsz, seq, d = x1.shape
    tm = PROJ_TM
    const = lambda b, i: (0, 0)
    whole = lambda a: pl.BlockSpec(a.shape, const, pipeline_mode=pl.Buffered(1))
    rows = lambda width: pl.BlockSpec((None, tm, width), lambda b, i: (b, i, 0))
    cols = lambda height, width=tm: pl.BlockSpec((None, height, width), lambda b, i: (b, 0, i))
    return pl.pallas_call(
        _inproj_kernel,
        grid=(bsz, seq // tm),
        in_specs=[rows(d),
                  pl.BlockSpec((None, N_MODS, d), lambda b, i: (b, 0, 0)),
                  whole(w_main), whole(w_main_t), whole(w_f), whole(b_f)],
        out_specs=[cols(D_SWA // SWA_PAIR, SWA_PAIR * tm), rows(D_SWA_SLOTS), cols(D_SWA_KV),
                   cols(D_FOX_SLOTS), rows(D_FOX_SLOTS),
                   pl.BlockSpec((None, tm // ATT_TK, D_FOX_V, ATT_TK), lambda b, i: (b, i, 0, 0))],
        out_shape=[jax.ShapeDtypeStruct((bsz, D_SWA // SWA_PAIR, SWA_PAIR * seq), BF16),
                   jax.ShapeDtypeStruct((bsz, seq, D_SWA_SLOTS), BF16),
                   jax.ShapeDtypeStruct((bsz, D_SWA_KV, seq), BF16),
                   jax.ShapeDtypeStruct((bsz, D_FOX_SLOTS, seq), BF16),
                   jax.ShapeDtypeStruct((bsz, seq, D_FOX_SLOTS), BF16),
                   jax.ShapeDtypeStruct((bsz, seq // ATT_TK, D_FOX_V, ATT_TK), BF16)],
        scratch_shapes=[pltpu.VMEM((SUBLANES, 1), F32)],
        compiler_params=pltpu.CompilerParams(dimension_semantics=("arbitrary", "arbitrary"),
                                             vmem_limit_bytes=VMEM_LIMIT),
        name="in_proj",
    )(x1, mods, w_main, w_main_t, w_f, b_f)


def _split3_host(x):
    x = np.asarray(x, np.float32)
    terms = []
    for _ in range(N_SPLIT):
        t = x.astype(jnp.bfloat16).astype(np.float32)
        terms.append(t)
        x = x - t
    return np.stack(terms)


def _alibi_rows():
    rows = np.zeros((SWA_Q_HEADS // SWA_PAIR, 2, HEAD_DIM, SWA_PAIR * BLOCK), np.float32)
    t = np.arange(BLOCK, dtype=np.float32)
    for hh in range(SWA_Q_HEADS):
        pair, e = hh // SWA_PAIR, hh % SWA_PAIR
        c = np.float32(2.0 ** (-8.0 * (hh + 1) / SWA_Q_HEADS) * LOG2_E)
        cols = slice(e * BLOCK, (e + 1) * BLOCK)
        for pq in range(2):
            for base, coeff in ((A_POS, c + 0 * t), (A_ONE, -c * (t + BLOCK * pq)),
                                (A_PAR, -c * BLOCK * (1 - 2 * pq) + 0 * t)):
                rows[pair, pq, base - HEAD_DIM:base - HEAD_DIM + N_SPLIT, cols] = _split3_host(coeff)
    return jnp.asarray(rows, BF16)


def _window_bias():
    kpos = np.arange(2 * BLOCK)[:, None] - BLOCK
    qpos = np.arange(SWA_PAIR * BLOCK)[None, :] % BLOCK
    dist = qpos - kpos
    inside = (dist >= 0) & (dist < WINDOW)
    both = np.stack([inside, inside & (kpos >= 0)])
    return jnp.asarray(np.where(both, 0.0, NEG_BIG), F32)


def _attn_kernel(sinks_ref, x1_ref, mods_ref, qat_ref, kac_ref, kap_ref, vatc_ref, vatp_ref,
                 alibi_ref, winbias_ref, winbias0_ref, qbt_ref, kb_ref, vbt_ref, gain_ref, wout_ref,
                 lng_ref, lnb_ref, o_ref, m_sc, acc_sc, s_sc, ot_sc):
    i = pl.program_id(1)
    tq, tk = ATT_TQ, ATT_TK

    pw = SWA_PAIR * BLOCK
    head_of_lane = lax.broadcasted_iota(jnp.int32, (1, pw), 1) // BLOCK
    ones_rows = (lax.broadcasted_iota(jnp.int32, (V_ROWS - HEAD_DIM, 2 * BLOCK), 0) == 0).astype(BF16)
    units = [(nb, g, c) for nb in range(tq // BLOCK) for g in range(SWA_KV_HEADS)
             for c in range(SWA_GROUP // SWA_PAIR)]

    def swa_scores(nb, g, c):
        r0 = nb * BLOCK
        pair = (g * SWA_GROUP) // SWA_PAIR + c
        k_prev = kap_ref[:, g * SLOT:(g + 1) * SLOT] if nb == 0 else (
            kac_ref[r0 - BLOCK:r0, g * SLOT:(g + 1) * SLOT])
        kk = jnp.concatenate([k_prev, kac_ref[r0:r0 + BLOCK, g * SLOT:(g + 1) * SLOT]], axis=0)
        qop = jnp.concatenate([qat_ref[pair * HEAD_DIM:(pair + 1) * HEAD_DIM, nb * pw:(nb + 1) * pw],
                               alibi_ref[pair, nb % 2]], axis=0)
        return jnp.dot(kk, qop, preferred_element_type=F32)

    def fox_scores(j, hh, c0=0):
        return jnp.dot(kb_ref[j, :, hh * SLOT:(hh + 1) * SLOT],
                       qbt_ref[hh * SLOT:(hh + 1) * SLOT, c0:],
                       preferred_element_type=F32)

    for u in range(FOX_LOOKAHEAD):
        s_sc[u, :, 0:pw] = swa_scores(*units[u])
    for u, (nb, g, c) in enumerate(units):
        s = s_sc[u % FOX_LOOKAHEAD, :, 0:pw]
        ahead = u + FOX_LOOKAHEAD
        if ahead < len(units):
            s_sc[ahead % FOX_LOOKAHEAD, :, 0:pw] = swa_scores(*units[ahead])
        else:
            s_sc[ahead % FOX_LOOKAHEAD] = fox_scores(0, ahead - len(units))
        r0 = nb * BLOCK
        h0 = g * SWA_GROUP + c * SWA_PAIR
        s = s + (winbias0_ref[...] if nb == 0 else winbias_ref[...])
        sink = jnp.zeros((1, pw), F32)
        for e in range(SWA_PAIR):
            sink = jnp.where(head_of_lane == e, sinks_ref[h0 + e] * LOG2_E, sink)
        m = jnp.maximum(jnp.max(s, axis=0, keepdims=True), sink)
        p = jnp.exp2(s - m)
        v_prev = vatp_ref[g * HEAD_DIM:(g + 1) * HEAD_DIM, :] if nb == 0 else (
            vatc_ref[g * HEAD_DIM:(g + 1) * HEAD_DIM, r0 - BLOCK:r0])
        vv = jnp.concatenate(
            [v_prev, vatc_ref[g * HEAD_DIM:(g + 1) * HEAD_DIM, r0:r0 + BLOCK]], axis=1)
        pv = jnp.dot(jnp.concatenate([vv, ones_rows], axis=0), p.astype(BF16),
                     preferred_element_type=F32)
        denom = pv[HEAD_DIM:HEAD_DIM + 1, :] + jnp.exp2(sink - m)
        o = pv[0:HEAD_DIM, :] / denom
        for e in range(SWA_PAIR):
            ot_sc[(h0 + e) * HEAD_DIM:(h0 + e + 1) * HEAD_DIM, r0:r0 + BLOCK] = (
                o[:, e * BLOCK:(e + 1) * BLOCK])

    m_sc[...] = jnp.full(m_sc.shape, NEG_BIG, F32)
    acc_sc[...] = jnp.zeros(acc_sc.shape, F32)

    def fox_block(j, causal, next_c0, c0=0):
        pending = {}
        for hh in range(FOX_HEADS):
            s = s_sc[hh, :, c0:] if hh < FOX_LOOKAHEAD else pending.pop(hh)
            ahead = hh + FOX_LOOKAHEAD
            if ahead < FOX_HEADS:
                pending[ahead] = fox_scores(j, ahead, c0)
            elif next_c0 is not None:
                s_sc[ahead - FOX_HEADS, :, next_c0:] = fox_scores(j + 1, ahead - FOX_HEADS, next_c0)
            if causal is not None:
                s = jnp.where(causal[:, c0:], s, NEG_BIG)
            m_prev = m_sc[hh, :, c0:]
            m_new = jnp.maximum(m_prev, jnp.max(s, axis=0, keepdims=True))
            a = jnp.exp2(m_prev - m_new)
            p = jnp.exp2(s - m_new)
            acc_sc[hh, :, c0:] = a * acc_sc[hh, :, c0:] + jnp.dot(
                vbt_ref[j, hh * V_ROWS:(hh + 1) * V_ROWS, :], p.astype(BF16),
                preferred_element_type=F32)
            m_sc[hh, :, c0:] = m_new

    per_tile = tq // tk

    def body(jj, carry):
        for r in range(per_tile):
            fox_block(per_tile * jj + r, None, 0)
        return carry

    lax.fori_loop(0, i, body, 0)
    kk_i = lax.broadcasted_iota(jnp.int32, (tk, tq), 0)
    qq_i = lax.broadcasted_iota(jnp.int32, (tk, tq), 1)
    for r in range(per_tile):
        fox_block(per_tile * i + r, kk_i + r * tk <= qq_i,
                  (r + 1) * tk if r + 1 < per_tile else None, r * tk)

    for hh in range(FOX_HEADS):
        ot_sc[D_SWA + hh * HEAD_DIM:D_SWA + (hh + 1) * HEAD_DIM, :] = (
            acc_sc[hh, 0:HEAD_DIM, :] / acc_sc[hh, HEAD_DIM:HEAD_DIM + 1, :])

    o = ot_sc[...].T
    oa = o[:, 0:D_SWA]
    ob = o[:, D_SWA:D_SWA + D_FOX]
    oa = oa * lax.rsqrt(jnp.mean(oa * oa, axis=-1, keepdims=True) + RMS_EPS) * gain_ref[:, 0:D_SWA]
    ob = ob * lax.rsqrt(jnp.mean(ob * ob, axis=-1, keepdims=True) + RMS_EPS) * gain_ref[:, D_SWA:]
    on = jnp.concatenate([oa, ob], axis=-1).astype(BF16)
    y = jnp.dot(on, wout_ref[...], preferred_element_type=F32)
    gate = mods_ref[5:6, :]
    z = ALPHA * x1_ref[...] + (1.0 + gate) * y
    o_ref[...] = _layer_norm(z, lng_ref[...], lnb_ref[...])


def _attention(x1, mods, qat, ka, vat, qbt, kb, vbt, sinks, gain, w_out, ln_g, ln_b):
    bsz, seq, d = x1.shape
    tq = ATT_TQ
    blocks_per_tile = tq // BLOCK
    prev_blk = lambda i: jnp.maximum(i * blocks_per_tile - 1, 0)
    const = lambda b, i, s: (0, 0)
    kb_blocks = kb.reshape(bsz, seq // ATT_TK, ATT_TK, D_FOX_SLOTS)
    alibi = _alibi_rows()
    winbias = _window_bias()
    grid_spec = pltpu.PrefetchScalarGridSpec(
        num_scalar_prefetch=1,
        grid=(bsz, seq // tq),
        in_specs=[pl.BlockSpec((None, tq, d), lambda b, i, s: (b, i, 0)),
                  pl.BlockSpec((None, N_MODS, d), lambda b, i, s: (b, 0, 0)),
                  pl.BlockSpec((None, D_SWA // SWA_PAIR, SWA_PAIR * tq), lambda b, i, s: (b, 0, i)),
                  pl.BlockSpec((None, tq, D_SWA_SLOTS), lambda b, i, s: (b, i, 0)),
                  pl.BlockSpec((None, BLOCK, D_SWA_SLOTS), lambda b, i, s: (b, prev_blk(i), 0)),
                  pl.BlockSpec((None, D_SWA_KV, tq), lambda b, i, s: (b, 0, i)),
                  pl.BlockSpec((None, D_SWA_KV, BLOCK), lambda b, i, s: (b, 0, prev_blk(i))),
                  pl.BlockSpec(alibi.shape, lambda b, i, s: (0, 0, 0, 0)),
                  pl.BlockSpec((None,) + winbias.shape[1:], lambda b, i, s: (0, 0, 0)),
                  pl.BlockSpec((None,) + winbias.shape[1:],
                               lambda b, i, s: (jnp.where(i == 0, 1, 0), 0, 0)),
                  pl.BlockSpec((None, D_FOX_SLOTS, tq), lambda b, i, s: (b, 0, i)),
                  pl.BlockSpec((None, seq // ATT_TK, ATT_TK, D_FOX_SLOTS),
                               lambda b, i, s: (b, 0, 0, 0)),
                  pl.BlockSpec((None, seq // ATT_TK, D_FOX_V, ATT_TK),
                               lambda b, i, s: (b, 0, 0, 0)),
                  pl.BlockSpec((1, d), const),
                  pl.BlockSpec((d, d), const),
                  pl.BlockSpec((1, d), const),
                  pl.BlockSpec((1, d), const)],
        out_specs=pl.BlockSpec((None, tq, d), lambda b, i, s: (b, i, 0)),
        scratch_shapes=[pltpu.VMEM((FOX_HEADS, 1, tq), F32),
                        pltpu.VMEM((FOX_HEADS, V_ROWS, tq), F32),
                        pltpu.VMEM((FOX_LOOKAHEAD, ATT_TK, tq), F32),
                        pltpu.VMEM((d, tq), F32)],
    )
    return pl.pallas_call(
        _attn_kernel,
        grid_spec=grid_spec,
        out_shape=jax.ShapeDtypeStruct((bsz, seq, d), F32),
        compiler_params=pltpu.CompilerParams(dimension_semantics=("arbitrary", "arbitrary"),
                                             vmem_limit_bytes=VMEM_LIMIT),
        name="token_mix",
    )(sinks, x1, mods, qat, ka, ka, vat, vat, alibi, winbias, winbias, qbt, kb_blocks, vbt,
      gain.reshape(1, d), w_out, ln_g.reshape(1, d), ln_b.reshape(1, d))


def kernel(x, c, w_ada, b_ada, ffn1_w_gate, ffn1_w_up, ffn1_w_down, w_in, b_forget,
           swa_sinks, grp_gain, w_out, ffn2_w_gate, ffn2_w_up, ffn2_w_down,
           ln1_g, ln1_b, ln2_g, ln2_b, ln3_g, ln3_b):
    bsz, seq, d = x.shape
    assert w_ada.shape[0] == DEPTH == 1
    assert seq % FFN_TM == 0 and seq % PROJ_TM == 0 and seq % ATT_TQ == 0
    assert PROJ_TM % ATT_TK == 0 and ATT_TQ % ATT_TK == 0 and SWA_PAIR * BLOCK <= ATT_TQ
    assert w_in.shape[2] == N_MAIN + FOX_HEADS and D_SWA + D_FOX == d
    assert K_STRIDE == SUBLANES and K_GATE + N_SPLIT * K_STRIDE <= SLOT
    assert WINDOW == BLOCK and A_PAR + N_SPLIT <= SLOT
    assert PROJ_TM % (2 * BLOCK) == 0 and ATT_TQ % (2 * BLOCK) == 0
    l = 0
    mods = _mods(c, w_ada[l], b_ada[l]).reshape(bsz, N_MODS, d)

    x1 = _ffn(x, mods, ffn1_w_gate[l].astype(BF16), ffn1_w_up[l].astype(BF16),
              ffn1_w_down[l].astype(BF16), ln1_g[l], ln1_b[l], mod_base=0)

    w_main = w_in[l][:, :N_MAIN].astype(BF16)
    w_f = jnp.concatenate(
        [w_in[l][:, OFF_KA:OFF_KA + D_SWA_KV],
         jnp.pad(w_in[l][:, N_MAIN:], ((0, 0), (0, LANES - FOX_HEADS)))], axis=1).astype(BF16)
    b_f = jnp.pad(b_forget[l], (0, LANES - FOX_HEADS)).reshape(1, LANES)
    qat, ka, vat, qbt, kb, vbt = _inproj(x1, mods, w_main, w_main.T, w_f, b_f)

    x2 = _attention(x1, mods, qat, ka, vat, qbt, kb, vbt, swa_sinks[l],
                    grp_gain[l], w_out[l].astype(BF16), ln2_g[l], ln2_b[l])

    return _ffn(x2, mods, ffn2_w_gate[l].astype(BF16), ffn2_w_up[l].astype(BF16),
                ffn2_w_down[l].astype(BF16), ln3_g[l], ln3_b[l], mod_base=6)
```

```python
import functools
import math

import numpy as np
import jax
import jax.numpy as jnp
from jax import lax
from jax.experimental import pallas as pl
from jax.experimental.pallas import tpu as pltpu

F32 = jnp.float32
BF16 = jnp.bfloat16

SWA_Q_HEADS = 8
SWA_KV_HEADS = 2
SWA_GROUP = SWA_Q_HEADS // SWA_KV_HEADS
FOX_HEADS = 8
HEAD_DIM = 64
D_SWA = SWA_Q_HEADS * HEAD_DIM
D_SWA_KV = SWA_KV_HEADS * HEAD_DIM
D_FOX = FOX_HEADS * HEAD_DIM
WINDOW = 128
BLOCK = 128
MACARON = 0.5
DEPTH = 1
ALPHA = (2.0 * DEPTH) ** 0.25
LN_EPS = 1e-5
RMS_EPS = 1e-6
N_MODS = 9
QK_SCALE = 1.0 / math.sqrt(HEAD_DIM)

LANES = 128
SUBLANES = 8
VMEM_LIMIT = 56 * 1024 * 1024

NEG_BIG = -1e30

FFN_TM = 1024
FFN_TF = 256
FFN_PASSES = 4
PROJ_TM = 1024
ATT_TQ = 512
ATT_TK = 256
FOX_LOOKAHEAD = 8

SLOT = LANES
N_SPLIT = 3
Q_GATE = HEAD_DIM
K_GATE = HEAD_DIM + N_SPLIT
K_STRIDE = FOX_HEADS
D_FOX_SLOTS = FOX_HEADS * SLOT
BF16_ROWS = 16
V_ROWS = HEAD_DIM + BF16_ROWS
D_FOX_V = FOX_HEADS * V_ROWS
LOG2_E = math.log2(math.e)
A_POS = HEAD_DIM
A_ONE = A_POS + N_SPLIT
A_PAR = A_ONE + N_SPLIT
D_SWA_SLOTS = SWA_KV_HEADS * SLOT
SWA_PAIR = 2

OFF_QA = 0
OFF_KA = OFF_QA + D_SWA
OFF_VA = OFF_KA + D_SWA_KV
OFF_QB = OFF_VA + D_SWA_KV
OFF_KB = OFF_QB + D_FOX
OFF_VB = OFF_KB + D_FOX
N_MAIN = OFF_VB + D_FOX
T_QA = 0
T_VA = T_QA + D_SWA
T_QB = T_VA + D_SWA_KV
T_VB = T_QB + D_FOX
N_TRANSPOSED = T_VB + D_FOX


def _layer_norm(z, g, b):
    mu = jnp.mean(z, axis=-1, keepdims=True)
    zc = z - mu
    var = jnp.mean(zc * zc, axis=-1, keepdims=True)
    return zc * lax.rsqrt(var + LN_EPS) * g + b


def _nt_dot(a, b):
    return lax.dot_general(a, b, (((1,), (1,)), ((), ())), preferred_element_type=F32)


def _split3(x):
    hi = x.astype(BF16).astype(F32)
    r = x - hi
    mid = r.astype(BF16).astype(F32)
    lo = (r - mid).astype(BF16).astype(F32)
    return hi, mid, lo


def _mods_kernel(c_ref, w_ref, b_ref, o_ref):
    c = c_ref[...]
    sc = c * jax.nn.sigmoid(c)
    o_ref[...] = jnp.dot(sc, w_ref[...], preferred_element_type=F32,
                         precision=lax.Precision.HIGHEST) + b_ref[...]


def _mods(c, w_ada, b_ada):
    bsz, d = c.shape
    n = w_ada.shape[1]
    tn = 1024
    return pl.pallas_call(
        _mods_kernel,
        grid=(n // tn,),
        in_specs=[pl.BlockSpec((bsz, d), lambda j: (0, 0)),
                  pl.BlockSpec((d, tn), lambda j: (0, j)),
                  pl.BlockSpec((1, tn), lambda j: (0, j))],
        out_specs=pl.BlockSpec((bsz, tn), lambda j: (0, j)),
        out_shape=jax.ShapeDtypeStruct((bsz, n), F32),
        compiler_params=pltpu.CompilerParams(dimension_semantics=("arbitrary",),
                                             vmem_limit_bytes=VMEM_LIMIT),
        name="adaln_mods",
    )(c, w_ada, b_ada.reshape(1, n))


def _ffn_kernel(x_ref, mods_ref, wg_ref, wu_ref, wd_ref, lng_ref, lnb_ref, o_ref,
                h_ref, a_ref, *, mod_base):
    shift = mods_ref[mod_base:mod_base + 1, :]
    scale = mods_ref[mod_base + 1:mod_base + 2, :]
    gate = mods_ref[mod_base + 2:mod_base + 3, :]
    gate_half = (1.0 + gate) * MACARON
    d_ff = wg_ref.shape[1]
    rows = x_ref.shape[0] // FFN_PASSES
    for r0 in range(0, x_ref.shape[0], rows):
        x = x_ref[r0:r0 + rows, :]
        h_ref[r0:r0 + rows, :] = (x * (1.0 + scale) + shift).astype(BF16)
        for c0 in range(0, d_ff, FFN_TF):
            h = h_ref[r0:r0 + rows, :]
            g = jnp.dot(h, wg_ref[:, c0:c0 + FFN_TF], preferred_element_type=F32)
            u = jnp.dot(h, wu_ref[:, c0:c0 + FFN_TF], preferred_element_type=F32)
            a_ref[r0:r0 + rows, c0:c0 + FFN_TF] = (g * jax.nn.sigmoid(g) * u).astype(BF16)
        y = jnp.dot(a_ref[r0:r0 + rows, :], wd_ref[...], preferred_element_type=F32)
        z = ALPHA * x + gate_half * y
        o_ref[r0:r0 + rows, :] = _layer_norm(z, lng_ref[...], lnb_ref[...])


def _ffn(x, mods, wg, wu, wd, ln_g, ln_b, mod_base):
    bsz, seq, d = x.shape
    d_ff = wg.shape[1]
    const = lambda b, i: (0, 0)
    return pl.pallas_call(
        functools.partial(_ffn_kernel, mod_base=mod_base),
        grid=(bsz, seq // FFN_TM),
        in_specs=[pl.BlockSpec((None, FFN_TM, d), lambda b, i: (b, i, 0)),
                  pl.BlockSpec((None, N_MODS, d), lambda b, i: (b, 0, 0)),
                  pl.BlockSpec((d, d_ff), const, pipeline_mode=pl.Buffered(1)),
                  pl.BlockSpec((d, d_ff), const, pipeline_mode=pl.Buffered(1)),
                  pl.BlockSpec((d_ff, d), const, pipeline_mode=pl.Buffered(1)),
                  pl.BlockSpec((1, d), const),
                  pl.BlockSpec((1, d), const)],
        out_specs=pl.BlockSpec((None, FFN_TM, d), lambda b, i: (b, i, 0)),
        out_shape=jax.ShapeDtypeStruct((bsz, seq, d), F32),
        scratch_shapes=[pltpu.VMEM((FFN_TM, d), BF16),
                        pltpu.VMEM((FFN_TM, d_ff), BF16)],
        compiler_params=pltpu.CompilerParams(dimension_semantics=("arbitrary", "arbitrary"),
                                             vmem_limit_bytes=VMEM_LIMIT),
        name=f"ffn_mod{mod_base}",
    )(x, mods, wg, wu, wd, ln_g.reshape(1, d), ln_b.reshape(1, d))


def _inproj_kernel(x_ref, mods_ref, w_ref, wt_ref, wf_ref, bf_ref,
                   qat_ref, ka_ref, vat_ref, qbt_ref, kb_ref, vbt_ref,
                   carry_ref):
    @pl.when(pl.program_id(1) == 0)
    def _():
        carry_ref[...] = jnp.zeros_like(carry_ref)

    x = x_ref[...]
    tm = x.shape[0]
    shift = mods_ref[3:4, :]
    scale = mods_ref[4:5, :]
    h = (x * (1.0 + scale) + shift).astype(BF16)

    kaf = jnp.dot(h, wf_ref[...], preferred_element_type=F32)
    ka = kaf[:, 0:D_SWA_KV]
    f = kaf[:, D_SWA_KV:D_SWA_KV + LANES] + bf_ref[...]
    f_t = f.T[0:SUBLANES, :]
    cum_t = jnp.minimum(f_t, 0.0) - jnp.log1p(jnp.exp(-jnp.abs(f_t)))
    pos_t = lax.broadcasted_iota(jnp.int32, cum_t.shape, 1)
    k = 1
    while k < tm:
        cum_t = cum_t + jnp.where(pos_t >= k, pltpu.roll(cum_t, k, axis=1), 0.0)
        k *= 2
    cum_t = cum_t + carry_ref[...]
    carry_ref[...] = cum_t[:, tm - 1:tm]
    cum_t = cum_t * LOG2_E
    cum = jnp.concatenate(
        [cum_t, jnp.zeros((LANES - SUBLANES, tm), F32)], axis=0).T

    proj_t = _nt_dot(wt_ref[...], h)
    qat = (proj_t[T_QA:T_QA + D_SWA, :] * (QK_SCALE * LOG2_E)).astype(BF16)
    pw = SWA_PAIR * BLOCK
    for hh in range(SWA_Q_HEADS):
        pair, e = hh // SWA_PAIR, hh % SWA_PAIR
        for blk in range(tm // BLOCK):
            qat_ref[pair * HEAD_DIM:(pair + 1) * HEAD_DIM,
                    blk * pw + e * BLOCK:blk * pw + (e + 1) * BLOCK] = (
                qat[hh * HEAD_DIM:(hh + 1) * HEAD_DIM, blk * BLOCK:(blk + 1) * BLOCK])
    lane = lax.broadcasted_iota(jnp.int32, (tm, LANES), 1)
    krow = lax.broadcasted_iota(jnp.int32, (tm, LANES), 0)
    pos = (krow % BLOCK).astype(F32)
    parity = ((krow // BLOCK) % 2).astype(F32)
    swa_tail = jnp.where(lane < A_ONE, pos,
                         jnp.where(lane < A_PAR, 1.0,
                                   jnp.where(lane < A_PAR + N_SPLIT, parity, 0.0)))
    for g, src in ((0, ka), (1, pltpu.roll(ka, HEAD_DIM, axis=1))):
        ka_ref[:, g * SLOT:(g + 1) * SLOT] = jnp.where(lane < HEAD_DIM, src, swa_tail).astype(BF16)
    vat_ref[...] = proj_t[T_VA:T_VA + D_SWA_KV, :].astype(BF16)

    qbt = proj_t[T_QB:T_QB + D_FOX, :] * (QK_SCALE * LOG2_E)
    q_hi, q_mid, q_lo = _split3(cum_t)
    sub = lax.broadcasted_iota(jnp.int32, (SUBLANES, tm), 0)
    first_one = K_GATE - HEAD_DIM
    one_row = jnp.where(sub == first_one, 1.0, 0.0)
    for hh in range(FOX_HEADS):
        extra = jnp.where(sub == 0, q_hi[hh:hh + 1, :],
                jnp.where(sub == 1, q_mid[hh:hh + 1, :],
                jnp.where(sub == 2, q_lo[hh:hh + 1, :], one_row)))
        tail = jnp.concatenate(
            [extra] + [one_row] * (N_SPLIT - 1)
            + [jnp.zeros((SLOT - HEAD_DIM - N_SPLIT * SUBLANES, tm), F32)], axis=0)
        qbt_ref[hh * SLOT:hh * SLOT + HEAD_DIM, :] = (
            qbt[hh * HEAD_DIM:(hh + 1) * HEAD_DIM, :].astype(BF16))
        qbt_ref[hh * SLOT + HEAD_DIM:(hh + 1) * SLOT, :] = tail.astype(BF16)

    kb = jnp.dot(h, w_ref[:, OFF_KB:OFF_KB + D_FOX], preferred_element_type=F32)
    k_hi, k_mid, k_lo = _split3(-cum)
    packed = jnp.where(lane < K_STRIDE, k_hi,
             jnp.where(lane < 2 * K_STRIDE, pltpu.roll(k_mid, K_STRIDE, axis=1),
             jnp.where(lane < 3 * K_STRIDE, pltpu.roll(k_lo, 2 * K_STRIDE, axis=1), 0.0)))
    gate_lane = ((lane >= K_GATE) & (lane < K_GATE + N_SPLIT * K_STRIDE)
                 & ((lane - K_GATE) % K_STRIDE == 0))
    ones_lane = jnp.where((lane >= Q_GATE) & (lane < Q_GATE + N_SPLIT), 1.0, 0.0)
    for pair in range(FOX_HEADS // 2):
        kp = kb[:, pair * LANES:(pair + 1) * LANES]
        odd = pltpu.roll(kp, HEAD_DIM, axis=1)
        for e, src in ((0, kp), (1, odd)):
            hh = 2 * pair + e
            gates = pltpu.roll(packed, K_GATE - hh, axis=1)
            tail = jnp.where(gate_lane, gates, ones_lane)
            kb_ref[:, hh * SLOT:(hh + 1) * SLOT] = jnp.where(lane < HEAD_DIM, src, tail).astype(BF16)

    vbt = proj_t[T_VB:T_VB + D_FOX, :].astype(BF16)
    ones_row = (lax.broadcasted_iota(jnp.int32, (V_ROWS - HEAD_DIM, ATT_TK), 0) == 0).astype(BF16)
    for r in range(tm // ATT_TK):
        for hh in range(FOX_HEADS):
            vbt_ref[r, hh * V_ROWS:hh * V_ROWS + HEAD_DIM, :] = (
                vbt[hh * HEAD_DIM:(hh + 1) * HEAD_DIM, r * ATT_TK:(r + 1) * ATT_TK])
            vbt_ref[r, hh * V_ROWS + HEAD_DIM:(hh + 1) * V_ROWS, :] = ones_row


def _inproj(x1, mods, w_main, w_main_t, w_f, b_f):
    bsz, seq, d = x1.shape
    tm = PROJ_TM
    const = lambda b, i: (0, 0)
    whole = lambda a: pl.BlockSpec(a.shape, const, pipeline_mode=pl.Buffered(1))
    rows = lambda width: pl.BlockSpec((None, tm, width), lambda b, i: (b, i, 0))
    cols = lambda height, width=tm: pl.BlockSpec((None, height, width), lambda b, i: (b, 0, i))
    return pl.pallas_call(
        _inproj_kernel,
        grid=(bsz, seq // tm),
        in_specs=[rows(d),
                  pl.BlockSpec((None, N_MODS, d), lambda b, i: (b, 0, 0)),
                  whole(w_main), whole(w_main_t), whole(w_f), whole(b_f)],
        out_specs=[cols(D_SWA // SWA_PAIR, SWA_PAIR * tm), rows(D_SWA_SLOTS), cols(D_SWA_KV),
                   cols(D_FOX_SLOTS), rows(D_FOX_SLOTS),
                   pl.BlockSpec((None, tm // ATT_TK, D_FOX_V, ATT_TK), lambda b, i: (b, i, 0, 0))],
        out_shape=[jax.ShapeDtypeStruct((bsz, D_SWA // SWA_PAIR, SWA_PAIR * seq), BF16),
                   jax.ShapeDtypeStruct((bsz, seq, D_SWA_SLOTS), BF16),
                   jax.ShapeDtypeStruct((bsz, D_SWA_KV, seq), BF16),
                   jax.ShapeDtypeStruct((bsz, D_FOX_SLOTS, seq), BF16),
                   jax.ShapeDtypeStruct((bsz, seq, D_FOX_SLOTS), BF16),
                   jax.ShapeDtypeStruct((bsz, seq // ATT_TK, D_FOX_V, ATT_TK), BF16)],
        scratch_shapes=[pltpu.VMEM((SUBLANES, 1), F32)],
        compiler_params=pltpu.CompilerParams(dimension_semantics=("arbitrary", "arbitrary"),
                                             vmem_limit_bytes=VMEM_LIMIT),
        name="in_proj",
    )(x1, mods, w_main, w_main_t, w_f, b_f)


def _split3_host(x):
    x = np.asarray(x, np.float32)
    terms = []
    for _ in range(N_SPLIT):
        t = x.astype(jnp.bfloat16).astype(np.float32)
        terms.append(t)
        x = x - t
    return np.stack(terms)


def _alibi_rows():
    rows = np.zeros((SWA_Q_HEADS // SWA_PAIR, 2, HEAD_DIM, SWA_PAIR * BLOCK), np.float32)
    t = np.arange(BLOCK, dtype=np.float32)
    for hh in range(SWA_Q_HEADS):
        pair, e = hh // SWA_PAIR, hh % SWA_PAIR
        c = np.float32(2.0 ** (-8.0 * (hh + 1) / SWA_Q_HEADS) * LOG2_E)
        cols = slice(e * BLOCK, (e + 1) * BLOCK)
        for pq in range(2):
            for base, coeff in ((A_POS, c + 0 * t), (A_ONE, -c * (t + BLOCK * pq)),
                                (A_PAR, -c * BLOCK * (1 - 2 * pq) + 0 * t)):
                rows[pair, pq, base - HEAD_DIM:base - HEAD_DIM + N_SPLIT, cols] = _split3_host(coeff)
    return jnp.asarray(rows, BF16)


def _window_bias():
    kpos = np.arange(2 * BLOCK)[:, None] - BLOCK
    qpos = np.arange(SWA_PAIR * BLOCK)[None, :] % BLOCK
    dist = qpos - kpos
    inside = (dist >= 0) & (dist < WINDOW)
    both = np.stack([inside, inside & (kpos >= 0)])
    return jnp.asarray(np.where(both, 0.0, NEG_BIG), F32)


def _attn_kernel(sinks_ref, x1_ref, mods_ref, qat_ref, kac_ref, kap_ref, vatc_ref, vatp_ref,
                 alibi_ref, winbias_ref, winbias0_ref, qbt_ref, kb_ref, vbt_ref, gain_ref, wout_ref,
                 lng_ref, lnb_ref, o_ref, m_sc, acc_sc, s_sc, ot_sc):
    i = pl.program_id(1)
    tq, tk = ATT_TQ, ATT_TK

    pw = SWA_PAIR * BLOCK
    head_of_lane = lax.broadcasted_iota(jnp.int32, (1, pw), 1) // BLOCK
    ones_rows = (lax.broadcasted_iota(jnp.int32, (V_ROWS - HEAD_DIM, 2 * BLOCK), 0) == 0).astype(BF16)
    units = [(nb, g, c) for nb in range(tq // BLOCK) for g in range(SWA_KV_HEADS)
             for c in range(SWA_GROUP // SWA_PAIR)]

    def swa_scores(nb, g, c):
        r0 = nb * BLOCK
        pair = (g * SWA_GROUP) // SWA_PAIR + c
        k_prev = kap_ref[:, g * SLOT:(g + 1) * SLOT] if nb == 0 else (
            kac_ref[r0 - BLOCK:r0, g * SLOT:(g + 1) * SLOT])
        kk = jnp.concatenate([k_prev, kac_ref[r0:r0 + BLOCK, g * SLOT:(g + 1) * SLOT]], axis=0)
        qop = jnp.concatenate([qat_ref[pair * HEAD_DIM:(pair + 1) * HEAD_DIM, nb * pw:(nb + 1) * pw],
                               alibi_ref[pair, nb % 2]], axis=0)
        return jnp.dot(kk, qop, preferred_element_type=F32)

    def fox_scores(j, hh, c0=0):
        return jnp.dot(kb_ref[j, :, hh * SLOT:(hh + 1) * SLOT],
                       qbt_ref[hh * SLOT:(hh + 1) * SLOT, c0:],
                       preferred_element_type=F32)

    for u in range(FOX_LOOKAHEAD):
        s_sc[u, :, 0:pw] = swa_scores(*units[u])
    for u, (nb, g, c) in enumerate(units):
        s = s_sc[u % FOX_LOOKAHEAD, :, 0:pw]
        ahead = u + FOX_LOOKAHEAD
        if ahead < len(units):
            s_sc[ahead % FOX_LOOKAHEAD, :, 0:pw] = swa_scores(*units[ahead])
        else:
            s_sc[ahead % FOX_LOOKAHEAD] = fox_scores(0, ahead - len(units))
        r0 = nb * BLOCK
        h0 = g * SWA_GROUP + c * SWA_PAIR
        s = s + (winbias0_ref[...] if nb == 0 else winbias_ref[...])
        sink = jnp.zeros((1, pw), F32)
        for e in range(SWA_PAIR):
            sink = jnp.where(head_of_lane == e, sinks_ref[h0 + e] * LOG2_E, sink)
        m = jnp.maximum(jnp.max(s, axis=0, keepdims=True), sink)
        p = jnp.exp2(s - m)
        v_prev = vatp_ref[g * HEAD_DIM:(g + 1) * HEAD_DIM, :] if nb == 0 else (
            vatc_ref[g * HEAD_DIM:(g + 1) * HEAD_DIM, r0 - BLOCK:r0])
        vv = jnp.concatenate(
            [v_prev, vatc_ref[g * HEAD_DIM:(g + 1) * HEAD_DIM, r0:r0 + BLOCK]], axis=1)
        pv = jnp.dot(jnp.concatenate([vv, ones_rows], axis=0), p.astype(BF16),
                     preferred_element_type=F32)
        denom = pv[HEAD_DIM:HEAD_DIM + 1, :] + jnp.exp2(sink - m)
        o = pv[0:HEAD_DIM, :] / denom
        for e in range(SWA_PAIR):
            ot_sc[(h0 + e) * HEAD_DIM:(h0 + e + 1) * HEAD_DIM, r0:r0 + BLOCK] = (
                o[:, e * BLOCK:(e + 1) * BLOCK])

    m_sc[...] = jnp.full(m_sc.shape, NEG_BIG, F32)
    acc_sc[...] = jnp.zeros(acc_sc.shape, F32)

    def fox_block(j, causal, next_c0, c0=0):
        pending = {}
        for hh in range(FOX_HEADS):
            s = s_sc[hh, :, c0:] if hh < FOX_LOOKAHEAD else pending.pop(hh)
            ahead = hh + FOX_LOOKAHEAD
            if ahead < FOX_HEADS:
                pending[ahead] = fox_scores(j, ahead, c0)
            elif next_c0 is not None:
                s_sc[ahead - FOX_HEADS, :, next_c0:] = fox_scores(j + 1, ahead - FOX_HEADS, next_c0)
            if causal is not None:
                s = jnp.where(causal[:, c0:], s, NEG_BIG)
            m_prev = m_sc[hh, :, c0:]
            m_new = jnp.maximum(m_prev, jnp.max(s, axis=0, keepdims=True))
            a = jnp.exp2(m_prev - m_new)
            p = jnp.exp2(s - m_new)
            acc_sc[hh, :, c0:] = a * acc_sc[hh, :, c0:] + jnp.dot(
                vbt_ref[j, hh * V_ROWS:(hh + 1) * V_ROWS, :], p.astype(BF16),
                preferred_element_type=F32)
            m_sc[hh, :, c0:] = m_new

    per_tile = tq // tk

    def body(jj, carry):
        for r in range(per_tile):
            fox_block(per_tile * jj + r, None, 0)
        return carry

    lax.fori_loop(0, i, body, 0)
    kk_i = lax.broadcasted_iota(jnp.int32, (tk, tq), 0)
    qq_i = lax.broadcasted_iota(jnp.int32, (tk, tq), 1)
    for r in range(per_tile):
        fox_block(per_tile * i + r, kk_i + r * tk <= qq_i,
                  (r + 1) * tk if r + 1 < per_tile else None, r * tk)

    for hh in range(FOX_HEADS):
        ot_sc[D_SWA + hh * HEAD_DIM:D_SWA + (hh + 1) * HEAD_DIM, :] = (
            acc_sc[hh, 0:HEAD_DIM, :] / acc_sc[hh, HEAD_DIM:HEAD_DIM + 1, :])

    o = ot_sc[...].T
    oa = o[:, 0:D_SWA]
    ob = o[:, D_SWA:D_SWA + D_FOX]
    oa = oa * lax.rsqrt(jnp.mean(oa * oa, axis=-1, keepdims=True) + RMS_EPS) * gain_ref[:, 0:D_SWA]
    ob = ob * lax.rsqrt(jnp.mean(ob * ob, axis=-1, keepdims=True) + RMS_EPS) * gain_ref[:, D_SWA:]
    on = jnp.concatenate([oa, ob], axis=-1).astype(BF16)
    y = jnp.dot(on, wout_ref[...], preferred_element_type=F32)
    gate = mods_ref[5:6, :]
    z = ALPHA * x1_ref[...] + (1.0 + gate) * y
    o_ref[...] = _layer_norm(z, lng_ref[...], lnb_ref[...])


def _attention(x1, mods, qat, ka, vat, qbt, kb, vbt, sinks, gain, w_out, ln_g, ln_b):
    bsz, seq, d = x1.shape
    tq = ATT_TQ
    blocks_per_tile = tq // BLOCK
    prev_blk = lambda i: jnp.maximum(i * blocks_per_tile - 1, 0)
    const = lambda b, i, s: (0, 0)
    kb_blocks = kb.reshape(bsz, seq // ATT_TK, ATT_TK, D_FOX_SLOTS)
    alibi = _alibi_rows()
    winbias = _window_bias()
    grid_spec = pltpu.PrefetchScalarGridSpec(
        num_scalar_prefetch=1,
        grid=(bsz, seq // tq),
        in_specs=[pl.BlockSpec((None, tq, d), lambda b, i, s: (b, i, 0)),
                  pl.BlockSpec((None, N_MODS, d), lambda b, i, s: (b, 0, 0)),
                  pl.BlockSpec((None, D_SWA // SWA_PAIR, SWA_PAIR * tq), lambda b, i, s: (b, 0, i)),
                  pl.BlockSpec((None, tq, D_SWA_SLOTS), lambda b, i, s: (b, i, 0)),
                  pl.BlockSpec((None, BLOCK, D_SWA_SLOTS), lambda b, i, s: (b, prev_blk(i), 0)),
                  pl.BlockSpec((None, D_SWA_KV, tq), lambda b, i, s: (b, 0, i)),
                  pl.BlockSpec((None, D_SWA_KV, BLOCK), lambda b, i, s: (b, 0, prev_blk(i))),
                  pl.BlockSpec(alibi.shape, lambda b, i, s: (0, 0, 0, 0)),
                  pl.BlockSpec((None,) + winbias.shape[1:], lambda b, i, s: (0, 0, 0)),
                  pl.BlockSpec((None,) + winbias.shape[1:],
                               lambda b, i, s: (jnp.where(i == 0, 1, 0), 0, 0)),
                  pl.BlockSpec((None, D_FOX_SLOTS, tq), lambda b, i, s: (b, 0, i)),
                  pl.BlockSpec((None, seq // ATT_TK, ATT_TK, D_FOX_SLOTS),
                               lambda b, i, s: (b, 0, 0, 0)),
                  pl.BlockSpec((None, seq // ATT_TK, D_FOX_V, ATT_TK),
                               lambda b, i, s: (b, 0, 0, 0)),
                  pl.BlockSpec((1, d), const),
                  pl.BlockSpec((d, d), const),
                  pl.BlockSpec((1, d), const),
                  pl.BlockSpec((1, d), const)],
        out_specs=pl.BlockSpec((None, tq, d), lambda b, i, s: (b, i, 0)),
        scratch_shapes=[pltpu.VMEM((FOX_HEADS, 1, tq), F32),
                        pltpu.VMEM((FOX_HEADS, V_ROWS, tq), F32),
                        pltpu.VMEM((FOX_LOOKAHEAD, ATT_TK, tq), F32),
                        pltpu.VMEM((d, tq), F32)],
    )
    return pl.pallas_call(
        _attn_kernel,
        grid_spec=grid_spec,
        out_shape=jax.ShapeDtypeStruct((bsz, seq, d), F32),
        compiler_params=pltpu.CompilerParams(dimension_semantics=("arbitrary", "arbitrary"),
                                             vmem_limit_bytes=VMEM_LIMIT),
        name="token_mix",
    )(sinks, x1, mods, qat, ka, ka, vat, vat, alibi, winbias, winbias, qbt, kb_blocks, vbt,
      gain.reshape(1, d), w_out, ln_g.reshape(1, d), ln_b.reshape(1, d))


def kernel(x, c, w_ada, b_ada, ffn1_w_gate, ffn1_w_up, ffn1_w_down, w_in, b_forget,
           swa_sinks, grp_gain, w_out, ffn2_w_gate, ffn2_w_up, ffn2_w_down,
           ln1_g, ln1_b, ln2_g, ln2_b, ln3_g, ln3_b):
    bsz, seq, d = x.shape
    assert w_ada.shape[0] == DEPTH == 1
    assert seq % FFN_TM == 0 and seq % PROJ_TM == 0 and seq % ATT_TQ == 0
    assert PROJ_TM % ATT_TK == 0 and ATT_TQ % ATT_TK == 0 and SWA_PAIR * BLOCK <= ATT_TQ
    assert w_in.shape[2] == N_MAIN + FOX_HEADS and D_SWA + D_FOX == d
    assert K_STRIDE == SUBLANES and K_GATE + N_SPLIT * K_STRIDE <= SLOT
    assert WINDOW == BLOCK and A_PAR + N_SPLIT <= SLOT
    assert PROJ_TM % (2 * BLOCK) == 0 and ATT_TQ % (2 * BLOCK) == 0
    l = 0
    mods = _mods(c, w_ada[l], b_ada[l]).reshape(bsz, N_MODS, d)

    x1 = _ffn(x, mods, ffn1_w_gate[l].astype(BF16), ffn1_w_up[l].astype(BF16),
              ffn1_w_down[l].astype(BF16), ln1_g[l], ln1_b[l], mod_base=0)

    w_main = w_in[l][:, :N_MAIN].astype(BF16)
    w_f = jnp.concatenate(
        [w_in[l][:, OFF_KA:OFF_KA + D_SWA_KV],
         jnp.pad(w_in[l][:, N_MAIN:], ((0, 0), (0, LANES - FOX_HEADS)))], axis=1).astype(BF16)
    b_f = jnp.pad(b_forget[l], (0, LANES - FOX_HEADS)).reshape(1, LANES)
    w_t = jnp.concatenate([w_in[l][:, o:o + n] for o, n in (
        (OFF_QA, D_SWA), (OFF_VA, D_SWA_KV), (OFF_QB, D_FOX), (OFF_VB, D_FOX))], axis=1).T.astype(BF16)
    qat, ka, vat, qbt, kb, vbt = _inproj(x1, mods, w_main, w_t, w_f, b_f)

    x2 = _attention(x1, mods, qat, ka, vat, qbt, kb, vbt, swa_sinks[l],
                    grp_gain[l], w_out[l].astype(BF16), ln2_g[l], ln2_b[l])

    return _ffn(x2, mods, ffn2_w_gate[l].astype(BF16), ffn2_w_up[l].astype(BF16),
                ffn2_w_down[l].astype(BF16), ln3_g[l], ln3_b[l], mod_base=6)
```
